```python
import jax, jax.numpy as jnp
from jax import lax
import numpy as np


D_MODEL = 2048
BATCH = 8
SEQ = 2048
DEPTH = 4
DEC_BATCH = 16
DEC_SEQ = 16
PAST_LEN = 1024

CHUNK = 64
P_DIM = 256
EPS = 1e-6

M_HEADS = 4
M_DV = D_MODEL // M_HEADS
M_DK = M_DV // 2
M_QK = M_HEADS * M_DK
M_V = M_HEADS * M_DV
CONV_W = 4

R_HEAD = 64
R_HEADS = D_MODEL // R_HEAD
R_W = R_HEADS * R_HEAD
LORA_W = 96
LORA_A = 96
LORA_G = 256
R_SHIFT = 3 * R_W + LORA_W + LORA_A + LORA_G
GN_EPS = 64e-5

D_FF = -(-8 * D_MODEL // (3 * 256)) * 256

Q_END = 2 * M_QK
V_END = Q_END + M_V
O_END = V_END + M_V
I_END = O_END + M_HEADS
F_END = I_END + M_HEADS
R_END = F_END + R_SHIFT
GA_END = R_END + D_MODEL
N_COLS = GA_END + D_MODEL
COL_SPLITS = (Q_END, V_END, O_END, I_END, F_END, R_END, GA_END)
R_SPLITS = (R_W, 2 * R_W, 3 * R_W, 3 * R_W + LORA_W, 3 * R_W + LORA_W + LORA_A)

kernel_name = 'mlstm_rwkv7_gated_streaming_encoder'


def _rmsnorm(x, g):
    xf = x.astype(jnp.float32)
    y = xf * lax.rsqrt(jnp.mean(xf * xf, axis=-1, keepdims=True) + EPS)
    return (y * g.astype(jnp.float32)).astype(x.dtype)


def _causal_conv(x, buf, w, b):
    T = x.shape[1]
    xp = jnp.concatenate([buf.astype(x.dtype), x], axis=1)
    y = b + sum(xp[:, j:j + T] * w[j] for j in range(CONV_W))
    return y, xp[:, xp.shape[1] - (CONV_W - 1):]


def _mlstm_chunkwise(q, k, v, li, lf, C0, n0, m0, chunk):
    B, H, T, DK = q.shape
    DV = v.shape[-1]
    nc = T // chunk

    def to_chunks(a):
        return jnp.moveaxis(a.reshape((B, H, nc, chunk) + a.shape[3:]), 2, 0)

    mask = jnp.tril(jnp.ones((chunk, chunk), dtype=bool))

    def step(carry, inp):
        C, n, m = carry
        qc, kc, vc, lic, lfc = inp
        b = jnp.cumsum(lfc, axis=-1)
        a = b + m[..., None]
        dmat = jnp.where(mask, b[..., :, None] - b[..., None, :] + lic[..., None, :], -jnp.inf)
        mt = jnp.maximum(a, jnp.max(dmat, axis=-1))
        ea = jnp.exp(a - mt)
        p = jnp.exp(dmat - mt[..., None])
        s = jnp.einsum('bhtk,bhsk->bhts', qc, kc) * p
        num = ea[..., None] * jnp.einsum('bhtk,bhkv->bhtv', qc, C) + jnp.einsum('bhts,bhsv->bhtv', s, vc)
        den = ea * jnp.einsum('bhtk,bhk->bht', qc, n) + jnp.sum(s, axis=-1)
        h = num / jnp.maximum(jnp.abs(den), jnp.exp(-mt))[..., None]
        m_new = mt[..., -1]
        dec = jnp.exp(a[..., -1] - m_new)
        wk = jnp.exp(b[..., -1:] - b + lic - m_new[..., None])
        C_new = dec[..., None, None] * C + jnp.einsum('bhs,bhsk,bhsv->bhkv', wk, kc, vc)
        n_new = dec[..., None] * n + jnp.einsum('bhs,bhsk->bhk', wk, kc)
        return (C_new, n_new, m_new), h

    xs = (to_chunks(q), to_chunks(k), to_chunks(v), to_chunks(li), to_chunks(lf))
    (C, n, m), hs = lax.scan(step, (C0, n0, m0), xs)
    h = jnp.moveaxis(hs, 0, 2).reshape(B, H, T, DV)
    return h, C, n, m


def _rwkv7_scan(r, w, k, v, kk, a, S0):
    def step(S, inp):
        rt, wt, kt, vt, kkt, at = inp
        sa = jnp.einsum('bhvk,bhk->bhv', S, -kkt)
        S = S * wt[:, :, None, :] + sa[..., None] * (kkt * at)[:, :, None, :] + vt[..., None] * kt[:, :, None, :]
        return S, jnp.einsum('bhvk,bhk->bhv', S, rt)

    xs = tuple(jnp.moveaxis(t, 1, 0) for t in (r, w, k, v, kk, a))
    S, out = lax.scan(step, S0, xs)
    return jnp.moveaxis(out, 0, 1), S


def _layer(h, p_l, conv0, C0, n0, m0, shift0, S0, chunk,
           g_mix, w_in, conv_w, conv_b, b_i, b_f, m_norm_g,
           rwkv_mu, w0, w_w2, a0, w_a2, w_g2, k_k, k_a, r_k, gn_w, gn_b,
           w_out, g_ffn, w_gate, w_up, w_down, g_ple, w_ple_gate, w_ple_in):
    f32 = jnp.float32
    B, T, _ = h.shape
    u = _rmsnorm(h, g_mix)
    z = u @ w_in
    qk_pre, v_m, o_m, i_m, f_m, zr, ga, gb = jnp.split(z, COL_SPLITS, axis=-1)

    qk, conv_new = _causal_conv(qk_pre, conv0, conv_w, conv_b)
    qk = jax.nn.silu(qk)
    q, k = jnp.split(qk, 2, axis=-1)
    q = q.reshape(B, T, M_HEADS, M_DK).transpose(0, 2, 1, 3).astype(f32)
    k = k.reshape(B, T, M_HEADS, M_DK).transpose(0, 2, 1, 3).astype(f32) * (M_DK ** -0.5)
    v = v_m.reshape(B, T, M_HEADS, M_DV).transpose(0, 2, 1, 3).astype(f32)
    li = (i_m + b_i).astype(f32).transpose(0, 2, 1)
    lf = jax.nn.log_sigmoid((f_m + b_f).astype(f32)).transpose(0, 2, 1)
    hm, C1, n1, m1 = _mlstm_chunkwise(q, k, v, li, lf, C0.astype(f32), n0.astype(f32),
                                      m0.astype(f32), chunk)
    hm = hm.transpose(0, 2, 1, 3)
    hm = hm * lax.rsqrt(jnp.mean(hm * hm, axis=-1, keepdims=True) + EPS)
    y_a = jax.nn.sigmoid(o_m) * (hm.reshape(B, T, M_V) * m_norm_g).astype(h.dtype)

    zr_prev = jnp.concatenate([shift0.astype(zr.dtype), zr[:, :-1]], axis=1)
    shift_new = zr[:, -1:]
    zs = zr + rwkv_mu * (zr_prev - zr)
    r, kr, vr, xw, xa, xg = jnp.split(zs, R_SPLITS, axis=-1)
    wlog = -jax.nn.softplus(-(w0 + jnp.tanh(xw) @ w_w2).astype(f32)) - 0.5
    decay = jnp.exp(-jnp.exp(wlog))
    a = jax.nn.sigmoid((a0 + xa @ w_a2).astype(f32))
    g = jax.nn.sigmoid(xg) @ w_g2
    hs = (B, T, R_HEADS, R_HEAD)
    r = r.astype(f32).reshape(hs)
    kr = kr.astype(f32).reshape(hs)
    vr = vr.astype(f32).reshape(hs)
    a = a.reshape(hs)
    kk = kr * k_k.astype(f32).reshape(R_HEADS, R_HEAD)
    kk = kk / jnp.maximum(jnp.sqrt(jnp.sum(kk * kk, axis=-1, keepdims=True)), 1e-12)
    kr = kr * (1.0 + (a - 1.0) * k_a.astype(f32).reshape(R_HEADS, R_HEAD))
    o, S1 = _rwkv7_scan(r, decay.reshape(hs), kr, vr, kk, a, S0.astype(f32))
    mu_o = jnp.mean(o, axis=-1, keepdims=True)
    var_o = jnp.mean(jnp.square(o - mu_o), axis=-1, keepdims=True)
    o = ((o - mu_o) * lax.rsqrt(var_o + GN_EPS)).reshape(B, T, R_W) * gn_w + gn_b
    bonus = (jnp.sum(r * kr * r_k.astype(f32), axis=-1, keepdims=True) * vr).reshape(B, T, R_W)
    y_b = ((o + bonus) * g).astype(h.dtype)

    mix = jax.nn.sigmoid(ga) * y_a + jax.nn.sigmoid(gb) * y_b
    h = h + mix @ w_out

    u = _rmsnorm(h, g_ffn)
    h = h + (jax.nn.silu(u @ w_gate) * (u @ w_up)) @ w_down

    u = _rmsnorm(h, g_ple)
    h = h + jax.nn.sigmoid(u @ w_ple_gate) * (p_l @ w_ple_in)
    return h, (conv_new, C1, n1, m1, shift_new, S1)


def setup_inputs(seed: int = 0) -> dict:
    key = jax.random.key(seed)
    ks = iter(jax.random.split(key, 48))
    f32 = jnp.float32

    def nrm(shape, scale=1.0):
        return jax.random.normal(next(ks), shape, f32) * scale

    return {
        'x_prompt': nrm((BATCH, SEQ, D_MODEL)),
        'x_sample': nrm((DEC_BATCH, DEC_SEQ, D_MODEL)),
        'p_prompt': nrm((DEPTH, BATCH, SEQ, P_DIM)),
        'p_sample': nrm((DEPTH, DEC_BATCH, DEC_SEQ, P_DIM)),
        'state_mlstm_C': nrm((DEPTH, DEC_BATCH, M_HEADS, M_DK, M_DV), 0.1),
        'state_mlstm_n': nrm((DEPTH, DEC_BATCH, M_HEADS, M_DK), 0.1),
        'state_mlstm_m': nrm((DEPTH, DEC_BATCH, M_HEADS)),
        'state_mlstm_conv': nrm((DEPTH, DEC_BATCH, CONV_W - 1, 2 * M_QK)),
        'state_rwkv_wkv': nrm((DEPTH, DEC_BATCH, R_HEADS, R_HEAD, R_HEAD), 0.1),
        'state_rwkv_shift': nrm((DEPTH, DEC_BATCH, 1, R_SHIFT)),
        'g_mix': 1.0 + nrm((DEPTH, D_MODEL), 0.01),
        'w_in': nrm((DEPTH, D_MODEL, N_COLS), D_MODEL ** -0.5),
        'conv_w': nrm((DEPTH, CONV_W, 2 * M_QK), CONV_W ** -0.5),
        'conv_b': nrm((DEPTH, 2 * M_QK), 0.01),
        'b_i': nrm((DEPTH, M_HEADS), 0.1),
        'b_f': jnp.linspace(3.0, 6.0, M_HEADS, dtype=f32)[None, :] + nrm((DEPTH, M_HEADS), 0.1),
        'm_norm_g': 1.0 + nrm((DEPTH, M_V), 0.01),
        'rwkv_mu': jax.random.uniform(next(ks), (DEPTH, R_SHIFT), f32),
        'w0': jnp.linspace(-6.5, -1.5, R_W, dtype=f32)[None, :] + nrm((DEPTH, R_W), 0.05),
        'w_w2': nrm((DEPTH, LORA_W, R_W), LORA_W ** -0.5),
        'a0': nrm((DEPTH, R_W), 0.1),
        'w_a2': nrm((DEPTH, LORA_A, R_W), LORA_A ** -0.5),
        'w_g2': nrm((DEPTH, LORA_G, R_W), LORA_G ** -0.5),
        'k_k': 0.85 + nrm((DEPTH, R_W), 0.02),
        'k_a': 1.0 + nrm((DEPTH, R_W), 0.02),
        'r_k': -0.04 + nrm((DEPTH, R_HEADS, R_HEAD), 0.02),
        'gn_w': 1.0 + nrm((DEPTH, R_W), 0.01),
        'gn_b': nrm((DEPTH, R_W), 0.01),
        'w_out': nrm((DEPTH, D_MODEL, D_MODEL), D_MODEL ** -0.5),
        'g_ffn': 1.0 + nrm((DEPTH, D_MODEL), 0.01),
        'w_gate': nrm((DEPTH, D_MODEL, D_FF), D_MODEL ** -0.5),
        'w_up': nrm((DEPTH, D_MODEL, D_FF), D_MODEL ** -0.5),
        'w_down': nrm((DEPTH, D_FF, D_MODEL), D_FF ** -0.5),
        'g_ple': 1.0 + nrm((DEPTH, D_MODEL), 0.01),
        'w_ple_gate': nrm((DEPTH, D_MODEL, D_MODEL), D_MODEL ** -0.5),
        'w_ple_in': nrm((DEPTH, P_DIM, D_MODEL), P_DIM ** -0.5),
        'g_final': 1.0 + nrm((D_MODEL,), 0.01),
    }


def reference(x_prompt, x_sample, p_prompt, p_sample,
              state_mlstm_C, state_mlstm_n, state_mlstm_m, state_mlstm_conv,
              state_rwkv_wkv, state_rwkv_shift,
              g_mix, w_in, conv_w, conv_b, b_i, b_f, m_norm_g,
              rwkv_mu, w0, w_w2, a0, w_a2, w_g2, k_k, k_a, r_k, gn_w, gn_b,
              w_out, g_ffn, w_gate, w_up, w_down, g_ple, w_ple_gate, w_ple_in, g_final):
    def run(x, p, conv0, C0, n0, m0, shift0, S0, chunk):
        h = x
        outs = ([], [], [], [], [], [])
        for l in range(DEPTH):
            h, st = _layer(h, p[l], conv0[l], C0[l], n0[l], m0[l], shift0[l], S0[l], chunk,
                           g_mix[l], w_in[l], conv_w[l], conv_b[l], b_i[l], b_f[l], m_norm_g[l],
                           rwkv_mu[l], w0[l], w_w2[l], a0[l], w_a2[l], w_g2[l], k_k[l], k_a[l],
                           r_k[l], gn_w[l], gn_b[l], w_out[l], g_ffn[l], w_gate[l], w_up[l],
                           w_down[l], g_ple[l], w_ple_gate[l], w_ple_in[l])
            for lst, s in zip(outs, st):
                lst.append(s)
        stacked = [jnp.stack(o, axis=0) for o in outs]
        return _rmsnorm(h, g_final), stacked

    f32 = jnp.float32
    Bp = x_prompt.shape[0]
    y_prompt, (conv_p, C_p, n_p, m_p, shift_p, wkv_p) = run(
        x_prompt, p_prompt,
        jnp.zeros((DEPTH, Bp, CONV_W - 1, 2 * M_QK), x_prompt.dtype),
        jnp.zeros((DEPTH, Bp, M_HEADS, M_DK, M_DV), f32),
        jnp.zeros((DEPTH, Bp, M_HEADS, M_DK), f32),
        jnp.zeros((DEPTH, Bp, M_HEADS), f32),
        jnp.zeros((DEPTH, Bp, 1, R_SHIFT), x_prompt.dtype),
        jnp.zeros((DEPTH, Bp, R_HEADS, R_HEAD, R_HEAD), f32),
        CHUNK)
    y_sample, (conv_s, C_s, n_s, m_s, shift_s, wkv_s) = run(
        x_sample, p_sample, state_mlstm_conv, state_mlstm_C, state_mlstm_n, state_mlstm_m,
        state_rwkv_shift, state_rwkv_wkv, x_sample.shape[1])
    return (y_prompt, y_sample, C_p, n_p, m_p, conv_p, wkv_p, shift_p,
            C_s, n_s, m_s, conv_s, wkv_s, shift_s)
```

```python
import functools

import jax
import jax.numpy as jnp
from jax import lax
from jax.experimental import pallas as pl
from jax.experimental.pallas import tpu as pltpu

F32 = jnp.float32
BF16 = jnp.bfloat16

EPS = 1e-6
GN_EPS = 64e-5
M_HEADS = 4
CONV_W = 4
R_HEAD = 64
CHUNK = 64
LANES = 128
VMEM_LIMIT = 48 * 1024 * 1024


def _params(*sem):
    return pltpu.CompilerParams(dimension_semantics=sem, vmem_limit_bytes=VMEM_LIMIT)


def _rms(x, g):
    return x * lax.rsqrt(jnp.mean(x * x, axis=-1, keepdims=True) + EPS) * g


def _norm_mm_kernel(x_ref, g_ref, w_ref, o_ref, u_ref):
    @pl.when(pl.program_id(1) == 0)
    def _():
        u_ref[...] = _rms(x_ref[...], g_ref[...]).astype(BF16)

    o_ref[...] = jnp.dot(u_ref[...], w_ref[...], preferred_element_type=F32)


def norm_matmul(x, g, w, tm, tn):
    M, K = x.shape
    N = w.shape[1]
    return pl.pallas_call(
        _norm_mm_kernel,
        grid=(M // tm, N // tn),
        in_specs=[pl.BlockSpec((tm, K), lambda i, j: (i, 0)),
                  pl.BlockSpec((1, K), lambda i, j: (0, 0)),
                  pl.BlockSpec((K, tn), lambda i, j: (0, j))],
        out_specs=pl.BlockSpec((tm, tn), lambda i, j: (i, j)),
        out_shape=jax.ShapeDtypeStruct((M, N), F32),
        scratch_shapes=[pltpu.VMEM((tm, K), BF16)],
        compiler_params=_params("parallel", "arbitrary"),
        name="norm_matmul",
    )(x, g, w)


def _mm_kernel(a_ref, w_ref, o_ref):
    o_ref[...] = jnp.dot(a_ref[...].astype(BF16), w_ref[...], preferred_element_type=F32)


def matmul(a, w, tm):
    M, K = a.shape
    N = w.shape[1]
    return pl.pallas_call(
        _mm_kernel,
        grid=(M // tm,),
        in_specs=[pl.BlockSpec((tm, K), lambda i: (i, 0)),
                  pl.BlockSpec((K, N), lambda i: (0, 0))],
        out_specs=pl.BlockSpec((tm, N), lambda i: (i, 0)),
        out_shape=jax.ShapeDtypeStruct((M, N), F32),
        compiler_params=_params("parallel"),
        name="matmul",
    )(a, w)


def _mm_res_kernel(h_ref, a_ref, w_ref, o_ref):
    o_ref[...] = h_ref[...] + jnp.dot(a_ref[...].astype(BF16), w_ref[...],
                                      preferred_element_type=F32)


def matmul_residual(h, a, w, tm):
    M, K = a.shape
    N = w.shape[1]
    return pl.pallas_call(
        _mm_res_kernel,
        grid=(M // tm,),
        in_specs=[pl.BlockSpec((tm, N), lambda i: (i, 0)),
                  pl.BlockSpec((tm, K), lambda i: (i, 0)),
                  pl.BlockSpec((K, N), lambda i: (0, 0))],
        out_specs=pl.BlockSpec((tm, N), lambda i: (i, 0)),
        out_shape=jax.ShapeDtypeStruct((M, N), F32),
        compiler_params=_params("parallel"),
        name="matmul_residual",
    )(h, a, w)


def _ffn_kernel(h_ref, g_ref, wg_ref, wu_ref, wd_ref, o_ref, u_ref, acc_ref):
    f = pl.program_id(1)

    @pl.when(f == 0)
    def _():
        u_ref[...] = _rms(h_ref[...], g_ref[...]).astype(BF16)
        acc_ref[...] = jnp.zeros_like(acc_ref)

    u = u_ref[...]
    gate = jnp.dot(u, wg_ref[...], preferred_element_type=F32)
    up = jnp.dot(u, wu_ref[...], preferred_element_type=F32)
    act = (gate * jax.nn.sigmoid(gate) * up).astype(BF16)
    acc_ref[...] += jnp.dot(act, wd_ref[...], preferred_element_type=F32)

    @pl.when(f == pl.num_programs(1) - 1)
    def _():
        o_ref[...] = h_ref[...] + acc_ref[...]


def ffn(h, g, wg, wu, wd, tm, tf):
    M, D = h.shape
    F = wg.shape[1]
    return pl.pallas_call(
        _ffn_kernel,
        grid=(M // tm, F // tf),
        in_specs=[pl.BlockSpec((tm, D), lambda i, j: (i, 0)),
                  pl.BlockSpec((1, D), lambda i, j: (0, 0)),
                  pl.BlockSpec((D, tf), lambda i, j: (0, j)),
                  pl.BlockSpec((D, tf), lambda i, j: (0, j)),
                  pl.BlockSpec((tf, D), lambda i, j: (j, 0))],
        out_specs=pl.BlockSpec((tm, D), lambda i, j: (i, 0)),
        out_shape=jax.ShapeDtypeStruct((M, D), F32),
        scratch_shapes=[pltpu.VMEM((tm, D), BF16), pltpu.VMEM((tm, D), F32)],
        compiler_params=_params("parallel", "arbitrary"),
        name="ffn",
    )(h, g, wg, wu, wd)


def _ple_kernel(h_ref, g_ref, p_ref, wpg_ref, wpi_ref, gf_ref, o_ref, *, final):
    h = h_ref[...]
    u = _rms(h, g_ref[...]).astype(BF16)
    gate = jnp.dot(u, wpg_ref[...], preferred_element_type=F32)
    emb = jnp.dot(p_ref[...].astype(BF16), wpi_ref[...], preferred_element_type=F32)
    out = h + jax.nn.sigmoid(gate) * emb
    if final:
        out = _rms(out, gf_ref[...])
    o_ref[...] = out


def ple(h, g, p, wpg, wpi, g_final, tm, final):
    M, D = h.shape
    P = p.shape[1]
    return pl.pallas_call(
        functools.partial(_ple_kernel, final=final),
        grid=(M // tm,),
        in_specs=[pl.BlockSpec((tm, D), lambda i: (i, 0)),
                  pl.BlockSpec((1, D), lambda i: (0, 0)),
                  pl.BlockSpec((tm, P), lambda i: (i, 0)),
                  pl.BlockSpec((D, D), lambda i: (0, 0)),
                  pl.BlockSpec((P, D), lambda i: (0, 0)),
                  pl.BlockSpec((1, D), lambda i: (0, 0))],
        out_specs=pl.BlockSpec((tm, D), lambda i: (i, 0)),
        out_shape=jax.ShapeDtypeStruct((M, D), F32),
        compiler_params=_params("parallel"),
        name="ple",
    )(h, g, p, wpg, wpi, g_final)


def _mlstm_kernel(q_ref, k_ref, v_ref, lir_ref, lfr_ref, lic_ref, lfc_ref,
                  c0_ref, n0_ref, m0_ref,
                  h_ref, c1_ref, n1_ref, m1_ref,
                  C, n, m):
    c = pl.program_id(2)

    @pl.when(c == 0)
    def _():
        C[...] = c0_ref[...]
        n[...] = n0_ref[...]
        m[...] = m0_ref[...]

    L = q_ref.shape[0]
    q = q_ref[...]
    k = k_ref[...]
    vb = v_ref[...].astype(BF16)
    li_r = lir_ref[...]
    lf_r = lfr_ref[...]
    li_c = lic_ref[...]
    lf_c = lfc_ref[...]

    row = lax.broadcasted_iota(jnp.int32, (L, L), 0)
    col = lax.broadcasted_iota(jnp.int32, (L, L), 1)
    tril = row >= col
    b_col = jnp.sum(jnp.where(tril, lf_r, 0.0), axis=1, keepdims=True)
    b_row = jnp.sum(jnp.where(row <= col, lf_c, 0.0), axis=0, keepdims=True)
    a_col = b_col + m[...]
    dmat = jnp.where(tril, b_col - b_row + li_r, -jnp.inf)
    mt = jnp.maximum(a_col, jnp.max(dmat, axis=1, keepdims=True))
    ea = jnp.exp(a_col - mt)
    p = jnp.exp(dmat - mt)

    qb = q.astype(BF16)
    s = lax.dot_general(qb, k.astype(BF16), (((1,), (1,)), ((), ())),
                        preferred_element_type=F32) * p
    num = (ea * jnp.dot(qb, C[...].astype(BF16), preferred_element_type=F32)
           + jnp.dot(s.astype(BF16), vb, preferred_element_type=F32))
    den = ea * jnp.sum(q * n[...], axis=1, keepdims=True) + jnp.sum(s, axis=1, keepdims=True)
    hh = num / jnp.maximum(jnp.abs(den), jnp.exp(-mt))
    h_ref[...] = hh * lax.rsqrt(jnp.mean(hh * hh, axis=1, keepdims=True) + EPS)

    m_new = mt[L - 1:L, :]
    dec = jnp.exp(a_col[L - 1:L, :] - m_new)
    wk = jnp.exp(b_col[L - 1:L, :] - b_col + li_c - m_new)
    kw = k * wk
    C[...] = dec * C[...] + lax.dot_general(kw.astype(BF16), vb, (((0,), (0,)), ((), ())),
                                            preferred_element_type=F32)
    n[...] = dec * n[...] + jnp.sum(kw, axis=0, keepdims=True)
    m[...] = m_new

    @pl.when(c == pl.num_programs(2) - 1)
    def _():
        c1_ref[...] = C[...]
        n1_ref[...] = n[...]
        m1_ref[...] = m[...]


def mlstm(q, k, v, li, lf, C0, n0, m0, L):
    B, T, _ = q.shape
    H = M_HEADS
    DK = q.shape[2] // H
    DV = v.shape[2] // H
    nc = T // L

    def gates(a):
        a = a.transpose(0, 2, 1).reshape(B, H, nc, L)
        return a[:, :, :, None, :], a[:, :, :, :, None]

    li_r, li_c = gates(li)
    lf_r, lf_c = gates(lf)
    row_spec = pl.BlockSpec((None, None, None, 1, L), lambda b, h, c: (b, h, c, 0, 0))
    col_spec = pl.BlockSpec((None, None, None, L, 1), lambda b, h, c: (b, h, c, 0, 0))
    c_spec = pl.BlockSpec((None, None, DK, DV), lambda b, h, c: (b, h, 0, 0))
    n_spec = pl.BlockSpec((None, None, 1, DK), lambda b, h, c: (b, h, 0, 0))
    m_spec = pl.BlockSpec((None, None, 1, 1), lambda b, h, c: (b, h, 0, 0))
    h, C1, n1, m1 = pl.pallas_call(
        _mlstm_kernel,
        grid=(B, H, nc),
        in_specs=[pl.BlockSpec((None, L, DK), lambda b, h, c: (b, c, h)),
                  pl.BlockSpec((None, L, DK), lambda b, h, c: (b, c, h)),
                  pl.BlockSpec((None, L, DV), lambda b, h, c: (b, c, h)),
                  row_spec, row_spec, col_spec, col_spec, c_spec, n_spec, m_spec],
        out_specs=[pl.BlockSpec((None, L, DV), lambda b, h, c: (b, c, h)),
                   c_spec, n_spec, m_spec],
        out_shape=[jax.ShapeDtypeStruct((B, T, H * DV), F32),
                   jax.ShapeDtypeStruct((B, H, DK, DV), F32),
                   jax.ShapeDtypeStruct((B, H, 1, DK), F32),
                   jax.ShapeDtypeStruct((B, H, 1, 1), F32)],
        scratch_shapes=[pltpu.VMEM((DK, DV), F32), pltpu.VMEM((1, DK), F32),
                        pltpu.VMEM((1, 1), F32)],
        compiler_params=_params("parallel", "parallel", "arbitrary"),
        name="mlstm",
    )(q, k, v, li_r, lf_r, li_c, lf_c, C0, n0[:, :, None, :], m0[:, :, None, None])
    return h, C1, n1[:, :, 0, :], m1[:, :, 0, 0]


def _rwkv_kernel(r_ref, w_ref, k_ref, v_ref, kk_ref, a_ref, s0_ref,
                 o_ref, s1_ref, S, bb):
    tc = pl.program_id(1)

    @pl.when(tc == 0)
    def _():
        S[...] = s0_ref[...]

    steps, N, _ = r_ref.shape

    def step(t, carry):
        bb[...] = kk_ref[t] * a_ref[t]
        vt = v_ref[t]
        acc = [S[0] * kk_ref[t, 0:1, :], S[1] * kk_ref[t, 1:2, :]]
        for key in range(2, N):
            acc[key % 2] = acc[key % 2] + S[key] * kk_ref[t, key:key + 1, :]
        sa = -(acc[0] + acc[1])
        out = [None, None]
        for key in range(N):
            s_new = (S[key] * w_ref[t, key:key + 1, :] + sa * bb[key:key + 1, :]
                     + vt * k_ref[t, key:key + 1, :])
            S[key] = s_new
            term = s_new * r_ref[t, key:key + 1, :]
            out[key % 2] = term if out[key % 2] is None else out[key % 2] + term
        o_ref[t] = out[0] + out[1]
        return carry

    lax.fori_loop(0, steps, step, 0)

    @pl.when(tc == pl.num_programs(1) - 1)
    def _():
        s1_ref[...] = S[...]


def rwkv7(r, w, k, v, kk, a, S0, steps):
    B, T, H, N = r.shape
    BH = B * H

    def lanes_last(x):
        return x.transpose(1, 3, 0, 2).reshape(T, N, BH)

    seq_spec = pl.BlockSpec((steps, N, LANES), lambda g, t: (t, 0, g))
    st_spec = pl.BlockSpec((N, N, LANES), lambda g, t: (0, 0, g))
    o, S1 = pl.pallas_call(
        _rwkv_kernel,
        grid=(BH // LANES, T // steps),
        in_specs=[seq_spec] * 6 + [st_spec],
        out_specs=[seq_spec, st_spec],
        out_shape=[jax.ShapeDtypeStruct((T, N, BH), F32),
                   jax.ShapeDtypeStruct((N, N, BH), F32)],
        scratch_shapes=[pltpu.VMEM((N, N, LANES), F32), pltpu.VMEM((N, LANES), F32)],
        compiler_params=_params("parallel", "arbitrary"),
        name="rwkv7",
    )(*(lanes_last(x) for x in (r, w, k, v, kk, a)),
      S0.transpose(3, 2, 0, 1).reshape(N, N, BH))
    o = o.reshape(T, N, B, H).transpose(2, 0, 3, 1)
    S1 = S1.reshape(N, N, B, H).transpose(2, 3, 1, 0)
    return o, S1


def _pad_rows(w, mult):
    pad = (-w.shape[0]) % mult
    return jnp.pad(w, ((0, pad), (0, 0))) if pad else w


def _pad_cols(x, mult):
    pad = (-x.shape[-1]) % mult
    return jnp.pad(x, ((0, 0), (0, pad))) if pad else x


def _layer(h, p_l, conv0, C0, n0, m0, shift0, S0, chunk, W, final, g_final):
    B, T, D = h.shape
    M = B * T
    tm = min(512, M)
    DV = D // M_HEADS
    DK = DV // 2
    QK = M_HEADS * DK
    RH = D // R_HEAD
    lora_w, lora_a, lora_g = W["lora"]
    r_shift = 3 * D + lora_w + lora_a + lora_g
    q_end = 2 * QK
    v_end = q_end + D
    o_end = v_end + D
    i_end = o_end + M_HEADS
    f_end = i_end + M_HEADS
    r_end = f_end + r_shift
    ga_end = r_end + D
    n_cols = ga_end + D

    h2 = h.reshape(M, D)
    z = norm_matmul(h2, W["g_mix"], W["w_in"], tm, W["w_in_tn"])[:, :n_cols].reshape(B, T, n_cols)
    qk_pre = z[..., :q_end]
    v_m = z[..., q_end:v_end]
    o_m = z[..., v_end:o_end]
    i_m = z[..., o_end:i_end]
    f_m = z[..., i_end:f_end]
    zr = z[..., f_end:r_end]
    ga = z[..., r_end:ga_end]
    gb = z[..., ga_end:]

    xp = jnp.concatenate([conv0, qk_pre], axis=1)
    qk = W["conv_b"] + sum(xp[:, j:j + T] * W["conv_w"][j] for j in range(CONV_W))
    conv_new = xp[:, T:]
    qk = jax.nn.silu(qk)
    q = qk[..., :QK]
    k = qk[..., QK:] * (DK ** -0.5)
    li = i_m + W["b_i"]
    lf = jax.nn.log_sigmoid(f_m + W["b_f"])
    hm, C1, n1, m1 = mlstm(q, k, v_m, li, lf, C0, n0, m0, chunk)
    y_a = jax.nn.sigmoid(o_m) * (hm * W["m_norm_g"])

    zr_prev = jnp.concatenate([shift0, zr[:, :-1]], axis=1)
    shift_new = zr[:, -1:]
    zs = zr + W["rwkv_mu"] * (zr_prev - zr)
    r = zs[..., :D]
    kr = zs[..., D:2 * D]
    vr = zs[..., 2 * D:3 * D]
    xw = zs[..., 3 * D:3 * D + lora_w]
    xa = zs[..., 3 * D + lora_w:3 * D + lora_w + lora_a]
    xg = zs[..., 3 * D + lora_w + lora_a:]
    tw = _pad_cols(jnp.tanh(xw).reshape(M, lora_w), LANES)
    ta = _pad_cols(xa.reshape(M, lora_a), LANES)
    tg = _pad_cols(jax.nn.sigmoid(xg).reshape(M, lora_g), LANES)
    wlog = -jax.nn.softplus(-(W["w0"] + matmul(tw, W["w_w2"], tm))) - 0.5
    decay = jnp.exp(-jnp.exp(wlog))
    a = jax.nn.sigmoid(W["a0"] + matmul(ta, W["w_a2"], tm))
    g = matmul(tg, W["w_g2"], tm).reshape(B, T, D)
    hs = (B, T, RH, R_HEAD)
    r = r.reshape(hs)
    kr = kr.reshape(hs)
    vr = vr.reshape(hs)
    a = a.reshape(hs)
    kk = kr * W["k_k"].reshape(RH, R_HEAD)
    kk = kk / jnp.maximum(jnp.sqrt(jnp.sum(kk * kk, axis=-1, keepdims=True)), 1e-12)
    kr = kr * (1.0 + (a - 1.0) * W["k_a"].reshape(RH, R_HEAD))
    o, S1 = rwkv7(r, decay.reshape(hs), kr, vr, kk, a, S0, min(32, T))
    mu_o = jnp.mean(o, axis=-1, keepdims=True)
    var_o = jnp.mean(jnp.square(o - mu_o), axis=-1, keepdims=True)
    o = ((o - mu_o) * lax.rsqrt(var_o + GN_EPS)).reshape(B, T, D) * W["gn_w"] + W["gn_b"]
    bonus = (jnp.sum(r * kr * W["r_k"], axis=-1, keepdims=True) * vr).reshape(B, T, D)
    y_b = (o + bonus) * g

    mix = jax.nn.sigmoid(ga) * y_a + jax.nn.sigmoid(gb) * y_b
    h2 = matmul_residual(h2, mix.reshape(M, D), W["w_out"], tm)
    h2 = ffn(h2, W["g_ffn"], W["w_gate"], W["w_up"], W["w_down"], tm, W["ffn_tf"])
    h2 = ple(h2, W["g_ple"], p_l.reshape(M, -1), W["w_ple_gate"], W["w_ple_in"], g_final, tm, final)
    return h2.reshape(B, T, D), (conv_new, C1, n1, m1, shift_new, S1)


def _pick_tile(n, candidates):
    for c in candidates:
        if n % c == 0:
            return c
    return n


def kernel(x_prompt, x_sample, p_prompt, p_sample, state_mlstm_C, state_mlstm_n, state_mlstm_m, state_mlstm_conv, state_rwkv_wkv, state_rwkv_shift, g_mix, w_in, conv_w, conv_b, b_i, b_f, m_norm_g, rwkv_mu, w0, w_w2, a0, w_a2, w_g2, k_k, k_a, r_k, gn_w, gn_b, w_out, g_ffn, w_gate, w_up, w_down, g_ple, w_ple_gate, w_ple_in, g_final):
    depth = w_in.shape[0]
    D = x_prompt.shape[-1]
    DV = D // M_HEADS
    DK = DV // 2
    QK = M_HEADS * DK
    RH = D // R_HEAD
    r_shift = state_rwkv_shift.shape[-1]

    w_in_tn = 768
    layers = []
    for l in range(depth):
        w_in_l = w_in[l].astype(BF16)
        pad = (-w_in_l.shape[1]) % w_in_tn
        w_in_l = jnp.pad(w_in_l, ((0, 0), (0, pad)))
        layers.append(dict(
            g_mix=g_mix[l][None], w_in=w_in_l, w_in_tn=w_in_tn,
            lora=(w_w2.shape[1], w_a2.shape[1], w_g2.shape[1]),
            conv_w=conv_w[l], conv_b=conv_b[l], b_i=b_i[l], b_f=b_f[l], m_norm_g=m_norm_g[l],
            rwkv_mu=rwkv_mu[l], w0=w0[l],
            w_w2=_pad_rows(w_w2[l], LANES).astype(BF16), a0=a0[l],
            w_a2=_pad_rows(w_a2[l], LANES).astype(BF16),
            w_g2=_pad_rows(w_g2[l], LANES).astype(BF16),
            k_k=k_k[l], k_a=k_a[l], r_k=r_k[l], gn_w=gn_w[l], gn_b=gn_b[l],
            w_out=w_out[l].astype(BF16), g_ffn=g_ffn[l][None],
            w_gate=w_gate[l].astype(BF16), w_up=w_up[l].astype(BF16),
            w_down=w_down[l].astype(BF16), ffn_tf=_pick_tile(w_gate.shape[2], (512, 256, 128)),
            g_ple=g_ple[l][None], w_ple_gate=w_ple_gate[l].astype(BF16),
            w_ple_in=w_ple_in[l].astype(BF16)))
    gf = g_final[None]

    def run(x, p, conv0, C0, n0, m0, shift0, S0, chunk):
        h = x
        outs = ([], [], [], [], [], [])
        for l in range(depth):
            h, st = _layer(h, p[l], conv0[l], C0[l], n0[l], m0[l], shift0[l], S0[l], chunk,
                           layers[l], l == depth - 1, gf)
            for lst, s in zip(outs, st):
                lst.append(s)
        return h, [jnp.stack(o, axis=0) for o in outs]

    Bp = x_prompt.shape[0]
    y_prompt, (conv_p, C_p, n_p, m_p, shift_p, wkv_p) = run(
        x_prompt, p_prompt,
        jnp.zeros((depth, Bp, CONV_W - 1, 2 * QK), F32),
        jnp.zeros((depth, Bp, M_HEADS, DK, DV), F32),
        jnp.zeros((depth, Bp, M_HEADS, DK), F32),
        jnp.zeros((depth, Bp, M_HEADS), F32),
        jnp.zeros((depth, Bp, 1, r_shift), F32),
        jnp.zeros((depth, Bp, RH, R_HEAD, R_HEAD), F32),
        CHUNK)
    y_sample, (conv_s, C_s, n_s, m_s, shift_s, wkv_s) = run(
        x_sample, p_sample, state_mlstm_conv, state_mlstm_C, state_mlstm_n, state_mlstm_m,
        state_rwkv_shift, state_rwkv_wkv, x_sample.shape[1])
    return (y_prompt, y_sample, C_p, n_p, m_p, conv_p, wkv_p, shift_p,
            C_s, n_s, m_s, conv_s, wkv_s, shift_s)
```

```python
import functools

import jax
import jax.numpy as jnp
from jax import lax
from jax.experimental import pallas as pl
from jax.experimental.pallas import tpu as pltpu

F32 = jnp.float32
BF16 = jnp.bfloat16

EPS = 1e-6
GN_EPS = 64e-5
M_HEADS = 4
CONV_W = 4
R_HEAD = 64
CHUNK = 64
LANES = 128
SUBLANES = 8
VMEM_LIMIT = 48 * 1024 * 1024

GATE_BLOCK = 256


def _params(*sem):
    return pltpu.CompilerParams(dimension_semantics=sem, vmem_limit_bytes=VMEM_LIMIT)


def _rms(x, g):
    return x * lax.rsqrt(jnp.mean(x * x, axis=-1, keepdims=True) + EPS) * g


def _log_sigmoid(x):
    return jnp.minimum(x, 0.0) - jnp.log1p(jnp.exp(-jnp.abs(x)))


def _norm_mm_kernel(x_ref, g_ref, w_ref, o_ref, u_ref):
    @pl.when(pl.program_id(1) == 0)
    def _():
        u_ref[...] = _rms(x_ref[...], g_ref[...]).astype(BF16)

    o_ref[...] = jnp.dot(u_ref[...], w_ref[...], preferred_element_type=F32)


def norm_matmul(x, g, w, tm, tn):
    M, K = x.shape
    N = w.shape[1]
    return pl.pallas_call(
        _norm_mm_kernel,
        grid=(M // tm, N // tn),
        in_specs=[pl.BlockSpec((tm, K), lambda i, j: (i, 0)),
                  pl.BlockSpec((1, K), lambda i, j: (0, 0)),
                  pl.BlockSpec((K, tn), lambda i, j: (0, j))],
        out_specs=pl.BlockSpec((tm, tn), lambda i, j: (i, j)),
        out_shape=jax.ShapeDtypeStruct((M, N), F32),
        scratch_shapes=[pltpu.VMEM((tm, K), BF16)],
        compiler_params=_params("parallel", "arbitrary"),
        name="norm_matmul",
    )(x, g, w)


def _mm_kernel(a_ref, w_ref, o_ref):
    o_ref[...] = jnp.dot(a_ref[...].astype(BF16), w_ref[...], preferred_element_type=F32)


def matmul(a, w, tm):
    M, K = a.shape
    N = w.shape[1]
    return pl.pallas_call(
        _mm_kernel,
        grid=(M // tm,),
        in_specs=[pl.BlockSpec((tm, K), lambda i: (i, 0)),
                  pl.BlockSpec((K, N), lambda i: (0, 0))],
        out_specs=pl.BlockSpec((tm, N), lambda i: (i, 0)),
        out_shape=jax.ShapeDtypeStruct((M, N), F32),
        compiler_params=_params("parallel"),
        name="matmul",
    )(a, w)


def _merge_kernel(h_ref, ga_ref, gb_ref, ya_ref, yb_ref, w_ref, o_ref):
    mix = (jax.nn.sigmoid(ga_ref[...]) * ya_ref[...]
           + jax.nn.sigmoid(gb_ref[...]) * yb_ref[...])
    o_ref[...] = h_ref[...] + jnp.dot(mix.astype(BF16), w_ref[...], preferred_element_type=F32)


def merge_project(h, z, ya, yb, w, tm):
    M, D = h.shape
    row = pl.BlockSpec((tm, D), lambda i: (i, 0))
    return pl.pallas_call(
        _merge_kernel,
        grid=(M // tm,),
        in_specs=[row,
                  pl.BlockSpec((tm, D), lambda i: (i, 6)),
                  pl.BlockSpec((tm, D), lambda i: (i, 7)),
                  row, row,
                  pl.BlockSpec((D, D), lambda i: (0, 0), pipeline_mode=pl.Buffered(1))],
        out_specs=row,
        out_shape=jax.ShapeDtypeStruct((M, D), F32),
        compiler_params=_params("parallel"),
        name="merge_project",
    )(h, z, z, ya, yb, w)


def _ffn_kernel(h_ref, g_ref, wg_ref, wu_ref, wd_ref, o_ref, u_ref, acc_ref):
    f = pl.program_id(1)

    @pl.when(f == 0)
    def _():
        u_ref[...] = _rms(h_ref[...], g_ref[...]).astype(BF16)
        acc_ref[...] = jnp.zeros_like(acc_ref)

    u = u_ref[...]
    gate = jnp.dot(u, wg_ref[...], preferred_element_type=F32)
    up = jnp.dot(u, wu_ref[...], preferred_element_type=F32)
    act = (gate * jax.nn.sigmoid(gate) * up).astype(BF16)
    acc_ref[...] += jnp.dot(act, wd_ref[...], preferred_element_type=F32)

    @pl.when(f == pl.num_programs(1) - 1)
    def _():
        o_ref[...] = h_ref[...] + acc_ref[...]


def ffn(h, g, wg, wu, wd, tm, tf):
    M, D = h.shape
    F = wg.shape[1]
    return pl.pallas_call(
        _ffn_kernel,
        grid=(M // tm, F // tf),
        in_specs=[pl.BlockSpec((tm, D), lambda i, j: (i, 0)),
                  pl.BlockSpec((1, D), lambda i, j: (0, 0)),
                  pl.BlockSpec((D, tf), lambda i, j: (0, j)),
                  pl.BlockSpec((D, tf), lambda i, j: (0, j)),
                  pl.BlockSpec((tf, D), lambda i, j: (j, 0))],
        out_specs=pl.BlockSpec((tm, D), lambda i, j: (i, 0)),
        out_shape=jax.ShapeDtypeStruct((M, D), F32),
        scratch_shapes=[pltpu.VMEM((tm, D), BF16), pltpu.VMEM((tm, D), F32)],
        compiler_params=_params("parallel", "arbitrary"),
        name="ffn",
    )(h, g, wg, wu, wd)


def _ple_kernel(h_ref, g_ref, p_ref, wpg_ref, wpi_ref, gf_ref, o_ref, *, final):
    h = h_ref[...]
    u = _rms(h, g_ref[...]).astype(BF16)
    gate = jnp.dot(u, wpg_ref[...], preferred_element_type=F32)
    emb = jnp.dot(p_ref[...].astype(BF16), wpi_ref[...], preferred_element_type=F32)
    out = h + jax.nn.sigmoid(gate) * emb
    if final:
        out = _rms(out, gf_ref[...])
    o_ref[...] = out


def ple(h, g, p, wpg, wpi, g_final, tm, final):
    M, D = h.shape
    P = p.shape[1]
    return pl.pallas_call(
        functools.partial(_ple_kernel, final=final),
        grid=(M // tm,),
        in_specs=[pl.BlockSpec((tm, D), lambda i: (i, 0)),
                  pl.BlockSpec((1, D), lambda i: (0, 0)),
                  pl.BlockSpec((tm, P), lambda i: (i, 0)),
                  pl.BlockSpec((D, D), lambda i: (0, 0), pipeline_mode=pl.Buffered(1)),
                  pl.BlockSpec((P, D), lambda i: (0, 0), pipeline_mode=pl.Buffered(1)),
                  pl.BlockSpec((1, D), lambda i: (0, 0))],
        out_specs=pl.BlockSpec((tm, D), lambda i: (i, 0)),
        out_shape=jax.ShapeDtypeStruct((M, D), F32),
        compiler_params=_params("parallel"),
        name="ple",
    )(h, g, p, wpg, wpi, g_final)


def _mlstm_kernel(qk_ref, v_ref, o_ref, gt_ref, conv0_ref, cw_ref, cb_ref, gbias_ref, mg_ref,
                  c0_ref, n0_ref, m0_ref,
                  y_ref, convn_ref, c1_ref, n1_ref, m1_ref,
                  xp, C, n, m, *, gate_off):
    c = pl.program_id(1)
    L, D = qk_ref.shape
    H = M_HEADS
    DV = D // H
    DK = DV // 2
    QK = H * DK
    halo = CONV_W - 1
    top = SUBLANES - halo

    @pl.when(c == 0)
    def _():
        xp[top:SUBLANES, :] = conv0_ref[...]
        C[...] = c0_ref[...]
        n[...] = n0_ref[...]
        m[...] = m0_ref[...]

    xp[SUBLANES:SUBLANES + L, :] = qk_ref[...]
    gt = gt_ref[...] + gbias_ref[...]

    row = lax.broadcasted_iota(jnp.int32, (L, L), 0)
    col = lax.broadcasted_iota(jnp.int32, (L, L), 1)
    tril = row >= col
    eye = row == col

    def conv_silu(c0):
        acc = cb_ref[:, c0:c0 + DK] + xp[top:top + L, c0:c0 + DK] * cw_ref[0:1, c0:c0 + DK]
        for j in range(1, CONV_W):
            acc = acc + xp[top + j:top + j + L, c0:c0 + DK] * cw_ref[j:j + 1, c0:c0 + DK]
        return acc * jax.nn.sigmoid(acc)

    for h in range(H):
        q = conv_silu(h * DK)
        k = conv_silu(QK + h * DK) * (DK ** -0.5)
        vb = v_ref[:, h * DV:(h + 1) * DV].astype(BF16)
        li_c = gt[:, gate_off + h:gate_off + h + 1]
        lf_c = _log_sigmoid(gt[:, gate_off + H + h:gate_off + H + h + 1])
        li_r = jnp.sum(jnp.where(eye, li_c, 0.0), axis=0, keepdims=True)
        b_row = jnp.sum(jnp.where(row <= col, lf_c, 0.0), axis=0, keepdims=True)
        b_col = jnp.sum(jnp.where(eye, b_row, 0.0), axis=1, keepdims=True)
        m_old = m[h]
        a_col = b_col + m_old
        dmat = jnp.where(tril, b_col - b_row + li_r, -jnp.inf)
        mt = jnp.maximum(a_col, jnp.max(dmat, axis=1, keepdims=True))
        ea = jnp.exp(a_col - mt)
        p = jnp.exp(dmat - mt)

        qb = q.astype(BF16)
        Ch = C[h]
        nh = n[h]
        s = lax.dot_general(qb, k.astype(BF16), (((1,), (1,)), ((), ())),
                            preferred_element_type=F32) * p
        num = (ea * jnp.dot(qb, Ch.astype(BF16), preferred_element_type=F32)
               + jnp.dot(s.astype(BF16), vb, preferred_element_type=F32))
        den = ea * jnp.sum(q * nh, axis=1, keepdims=True) + jnp.sum(s, axis=1, keepdims=True)
        hh = num / jnp.maximum(jnp.abs(den), jnp.exp(-mt))
        hh = hh * lax.rsqrt(jnp.mean(hh * hh, axis=1, keepdims=True) + EPS)
        y_ref[:, h * DV:(h + 1) * DV] = (jax.nn.sigmoid(o_ref[:, h * DV:(h + 1) * DV])
                                         * (hh * mg_ref[:, h * DV:(h + 1) * DV]))

        m_new = mt[L - 1:L, :]
        dec = jnp.exp(a_col[L - 1:L, :] - m_new)
        wk = jnp.exp(b_col[L - 1:L, :] - b_col + li_c - m_new)
        kw = k * wk
        C[h] = dec * Ch + lax.dot_general(kw.astype(BF16), vb, (((0,), (0,)), ((), ())),
                                          preferred_element_type=F32)
        n[h] = dec * nh + jnp.sum(kw, axis=0, keepdims=True)
        m[h] = m_new

    xp[top:SUBLANES, :] = xp[top + L:SUBLANES + L, :]

    @pl.when(c == pl.num_programs(1) - 1)
    def _():
        convn_ref[...] = xp[top:SUBLANES, :]
        c1_ref[...] = C[...]
        n1_ref[...] = n[...]
        m1_ref[...] = m[...]


def mlstm_branch(z, conv0, C0, n0, m0, conv_w, conv_b, gate_bias, m_norm_g, L, gate_off):
    B, T, _ = z.shape
    H = M_HEADS
    _, _, DK, DV = C0.shape
    D = H * DV
    nc = T // L
    halo = CONV_W - 1

    def seq(j):
        return pl.BlockSpec((None, L, D), lambda b, c: (b, c, j))

    def per_batch(*shape):
        return pl.BlockSpec((None,) + shape, lambda b, c: (b,) + (0,) * len(shape))

    def const(*shape):
        return pl.BlockSpec(shape, lambda b, c: (0,) * len(shape))

    gate_blk = (8 * D + GATE_BLOCK) // GATE_BLOCK
    y, conv_new, C1, n1, m1 = pl.pallas_call(
        functools.partial(_mlstm_kernel, gate_off=gate_off),
        grid=(B, nc),
        in_specs=[seq(0), seq(1), seq(2),
                  pl.BlockSpec((None, L, GATE_BLOCK), lambda b, c: (b, c, gate_blk)),
                  per_batch(halo, D), const(CONV_W, D), const(1, D), const(1, GATE_BLOCK),
                  const(1, D),
                  per_batch(H, DK, DV), per_batch(H, 1, DK), per_batch(H, 1, 1)],
        out_specs=[seq(0), per_batch(halo, D),
                   per_batch(H, DK, DV), per_batch(H, 1, DK), per_batch(H, 1, 1)],
        out_shape=[jax.ShapeDtypeStruct((B, T, D), F32),
                   jax.ShapeDtypeStruct((B, halo, D), F32),
                   jax.ShapeDtypeStruct((B, H, DK, DV), F32),
                   jax.ShapeDtypeStruct((B, H, 1, DK), F32),
                   jax.ShapeDtypeStruct((B, H, 1, 1), F32)],
        scratch_shapes=[pltpu.VMEM((L + SUBLANES, D), F32), pltpu.VMEM((H, DK, DV), F32),
                        pltpu.VMEM((H, 1, DK), F32), pltpu.VMEM((H, 1, 1), F32)],
        compiler_params=_params("parallel", "arbitrary"),
        name="mlstm",
    )(z, z, z, z, conv0, conv_w, conv_b, gate_bias, m_norm_g,
      C0, n0[:, :, None, :], m0[:, :, None, None])
    return y, conv_new, C1, n1[:, :, 0, :], m1[:, :, 0, 0]


def _rwkv_kernel(r_ref, w_ref, k_ref, v_ref, kk_ref, a_ref, s0_ref,
                 o_ref, s1_ref, S, bb):
    tc = pl.program_id(1)

    @pl.when(tc == 0)
    def _():
        S[...] = s0_ref[...]

    steps, N, _ = r_ref.shape

    def step(t, carry):
        bb[...] = kk_ref[t] * a_ref[t]
        vt = v_ref[t]
        acc = [S[0] * kk_ref[t, 0:1, :], S[1] * kk_ref[t, 1:2, :]]
        for key in range(2, N):
            acc[key % 2] = acc[key % 2] + S[key] * kk_ref[t, key:key + 1, :]
        sa = -(acc[0] + acc[1])
        out = [None, None]
        for key in range(N):
            s_new = (S[key] * w_ref[t, key:key + 1, :] + sa * bb[key:key + 1, :]
                     + vt * k_ref[t, key:key + 1, :])
            S[key] = s_new
            term = s_new * r_ref[t, key:key + 1, :]
            out[key % 2] = term if out[key % 2] is None else out[key % 2] + term
        o_ref[t] = out[0] + out[1]
        return carry

    lax.fori_loop(0, steps, step, 0)

    @pl.when(tc == pl.num_programs(1) - 1)
    def _():
        s1_ref[...] = S[...]


def rwkv7(r, w, k, v, kk, a, S0, steps):
    B, T, H, N = r.shape
    BH = B * H

    def lanes_last(x):
        return x.transpose(1, 3, 0, 2).reshape(T, N, BH)

    seq_spec = pl.BlockSpec((steps, N, LANES), lambda g, t: (t, 0, g))
    st_spec = pl.BlockSpec((N, N, LANES), lambda g, t: (0, 0, g))
    o, S1 = pl.pallas_call(
        _rwkv_kernel,
        grid=(BH // LANES, T // steps),
        in_specs=[seq_spec] * 6 + [st_spec],
        out_specs=[seq_spec, st_spec],
        out_shape=[jax.ShapeDtypeStruct((T, N, BH), F32),
                   jax.ShapeDtypeStruct((N, N, BH), F32)],
        scratch_shapes=[pltpu.VMEM((N, N, LANES), F32), pltpu.VMEM((N, LANES), F32)],
        compiler_params=_params("parallel", "arbitrary"),
        name="rwkv7",
    )(*(lanes_last(x) for x in (r, w, k, v, kk, a)),
      S0.transpose(3, 2, 0, 1).reshape(N, N, BH))
    o = o.reshape(T, N, B, H).transpose(2, 0, 3, 1)
    S1 = S1.reshape(N, N, B, H).transpose(2, 3, 1, 0)
    return o, S1


def _pad_rows(w, mult):
    pad = (-w.shape[0]) % mult
    return jnp.pad(w, ((0, pad), (0, 0))) if pad else w


def _pad_cols(x, mult):
    pad = (-x.shape[-1]) % mult
    return jnp.pad(x, ((0, 0), (0, pad))) if pad else x


def _pick_tile(n, candidates):
    for c in candidates:
        if n % c == 0:
            return c
    return n


def _repack_w_in(w, D, lora_w, lora_a, lora_g):
    H = M_HEADS
    o_end = 3 * D
    f_end = o_end + 2 * H
    zr = f_end
    xw = zr + 3 * D
    xa = xw + lora_w
    xg = xa + lora_a
    ga = xg + lora_g
    small = lora_w + lora_a + 2 * H
    assert lora_g == GATE_BLOCK and small <= GATE_BLOCK
    pad = jnp.zeros((w.shape[0], GATE_BLOCK - small), w.dtype)
    return jnp.concatenate([
        w[:, :o_end], w[:, zr:xw], w[:, ga:ga + 2 * D], w[:, xg:ga],
        w[:, xw:xg], w[:, o_end:f_end], pad], axis=1)


def _layer(h, p_l, conv0, C0, n0, m0, shift0, S0, chunk, W, final, g_final):
    B, T, D = h.shape
    M = B * T
    tm = min(512, M)
    H = M_HEADS
    RH = D // R_HEAD
    lora_w, lora_a, lora_g = W["lora"]
    gate_off = lora_w + lora_a

    h2 = h.reshape(M, D)
    z2 = norm_matmul(h2, W["g_mix"], W["w_in"], min(1024, M), W["w_in_tn"])
    z = z2.reshape(B, T, -1)

    y_a, conv_new, C1, n1, m1 = mlstm_branch(z, conv0, C0, n0, m0, W["conv_w"], W["conv_b"],
                                             W["gate_bias"], W["m_norm_g"], chunk, gate_off)

    sm = 8 * D + GATE_BLOCK
    zr = jnp.concatenate([z[..., 3 * D:6 * D], z[..., sm:sm + gate_off], z[..., 8 * D:sm]], axis=-1)
    zr_prev = jnp.concatenate([shift0, zr[:, :-1]], axis=1)
    shift_new = zr[:, -1:]
    zs = zr + W["rwkv_mu"] * (zr_prev - zr)
    r = zs[..., :D]
    kr = zs[..., D:2 * D]
    vr = zs[..., 2 * D:3 * D]
    xw = zs[..., 3 * D:3 * D + lora_w]
    xa = zs[..., 3 * D + lora_w:3 * D + lora_w + lora_a]
    xg = zs[..., 3 * D + lora_w + lora_a:]
    tw = _pad_cols(jnp.tanh(xw).reshape(M, lora_w), LANES)
    ta = _pad_cols(xa.reshape(M, lora_a), LANES)
    tg = _pad_cols(jax.nn.sigmoid(xg).reshape(M, lora_g), LANES)
    wlog = -jax.nn.softplus(-(W["w0"] + matmul(tw, W["w_w2"], tm))) - 0.5
    decay = jnp.exp(-jnp.exp(wlog))
    a = jax.nn.sigmoid(W["a0"] + matmul(ta, W["w_a2"], tm))
    g = matmul(tg, W["w_g2"], tm).reshape(B, T, D)
    hs = (B, T, RH, R_HEAD)
    r = r.reshape(hs)
    kr = kr.reshape(hs)
    vr = vr.reshape(hs)
    a = a.reshape(hs)
    kk = kr * W["k_k"].reshape(RH, R_HEAD)
    kk = kk / jnp.maximum(jnp.sqrt(jnp.sum(kk * kk, axis=-1, keepdims=True)), 1e-12)
    kr = kr * (1.0 + (a - 1.0) * W["k_a"].reshape(RH, R_HEAD))
    o, S1 = rwkv7(r, decay.reshape(hs), kr, vr, kk, a, S0, min(32, T))
    mu_o = jnp.mean(o, axis=-1, keepdims=True)
    var_o = jnp.mean(jnp.square(o - mu_o), axis=-1, keepdims=True)
    o = ((o - mu_o) * lax.rsqrt(var_o + GN_EPS)).reshape(B, T, D) * W["gn_w"] + W["gn_b"]
    bonus = (jnp.sum(r * kr * W["r_k"], axis=-1, keepdims=True) * vr).reshape(B, T, D)
    y_b = (o + bonus) * g

    h2 = merge_project(h2, z2, y_a.reshape(M, D), y_b.reshape(M, D), W["w_out"], min(256, M))
    h2 = ffn(h2, W["g_ffn"], W["w_gate"], W["w_up"], W["w_down"], tm, W["ffn_tf"])
    h2 = ple(h2, W["g_ple"], p_l.reshape(M, -1), W["w_ple_gate"], W["w_ple_in"], g_final, tm, final)
    return h2.reshape(B, T, D), (conv_new, C1, n1, m1, shift_new, S1)


def kernel(x_prompt, x_sample, p_prompt, p_sample, state_mlstm_C, state_mlstm_n, state_mlstm_m, state_mlstm_conv, state_rwkv_wkv, state_rwkv_shift, g_mix, w_in, conv_w, conv_b, b_i, b_f, m_norm_g, rwkv_mu, w0, w_w2, a0, w_a2, w_g2, k_k, k_a, r_k, gn_w, gn_b, w_out, g_ffn, w_gate, w_up, w_down, g_ple, w_ple_gate, w_ple_in, g_final):
    depth = w_in.shape[0]
    D = x_prompt.shape[-1]
    H = M_HEADS
    DV = D // H
    DK = DV // 2
    QK = H * DK
    RH = D // R_HEAD
    r_shift = state_rwkv_shift.shape[-1]
    lora = (w_w2.shape[1], w_a2.shape[1], w_g2.shape[1])
    n_pack = 8 * D + 2 * GATE_BLOCK
    w_in_tn = _pick_tile(n_pack, (768, 512, 256, 128))

    layers = []
    for l in range(depth):
        gate_bias = jnp.zeros((1, GATE_BLOCK), F32)
        gate_bias = gate_bias.at[0, lora[0] + lora[1]:lora[0] + lora[1] + 2 * H].set(
            jnp.concatenate([b_i[l], b_f[l]]))
        layers.append(dict(
            g_mix=g_mix[l][None], w_in=_repack_w_in(w_in[l], D, *lora).astype(BF16),
            w_in_tn=w_in_tn, lora=lora,
            conv_w=conv_w[l], conv_b=conv_b[l][None], gate_bias=gate_bias,
            m_norm_g=m_norm_g[l][None],
            rwkv_mu=rwkv_mu[l], w0=w0[l],
            w_w2=_pad_rows(w_w2[l], LANES).astype(BF16), a0=a0[l],
            w_a2=_pad_rows(w_a2[l], LANES).astype(BF16),
            w_g2=_pad_rows(w_g2[l], LANES).astype(BF16),
            k_k=k_k[l], k_a=k_a[l], r_k=r_k[l], gn_w=gn_w[l], gn_b=gn_b[l],
            w_out=w_out[l].astype(BF16), g_ffn=g_ffn[l][None],
            w_gate=w_gate[l].astype(BF16), w_up=w_up[l].astype(BF16),
            w_down=w_down[l].astype(BF16), ffn_tf=_pick_tile(w_gate.shape[2], (512, 256, 128)),
            g_ple=g_ple[l][None], w_ple_gate=w_ple_gate[l].astype(BF16),
            w_ple_in=w_ple_in[l].astype(BF16)))
    gf = g_final[None]

    def run(x, p, conv0, C0, n0, m0, shift0, S0, chunk):
        h = x
        outs = ([], [], [], [], [], [])
        for l in range(depth):
            h, st = _layer(h, p[l], conv0[l], C0[l], n0[l], m0[l], shift0[l], S0[l], chunk,
                           layers[l], l == depth - 1, gf)
            for lst, s in zip(outs, st):
                lst.append(s)
        return h, [jnp.stack(o, axis=0) for o in outs]

    Bp = x_prompt.shape[0]
    y_prompt, (conv_p, C_p, n_p, m_p, shift_p, wkv_p) = run(
        x_prompt, p_prompt,
        jnp.zeros((depth, Bp, CONV_W - 1, 2 * QK), F32),
        jnp.zeros((depth, Bp, H, DK, DV), F32),
        jnp.zeros((depth, Bp, H, DK), F32),
        jnp.zeros((depth, Bp, H), F32),
        jnp.zeros((depth, Bp, 1, r_shift), F32),
        jnp.zeros((depth, Bp, RH, R_HEAD, R_HEAD), F32),
        CHUNK)
    y_sample, (conv_s, C_s, n_s, m_s, shift_s, wkv_s) = run(
        x_sample, p_sample, state_mlstm_conv, state_mlstm_C, state_mlstm_n, state_mlstm_m,
        state_rwkv_shift, state_rwkv_wkv, x_sample.shape[1])
    return (y_prompt, y_sample, C_p, n_p, m_p, conv_p, wkv_p, shift_p,
            C_s, n_s, m_s, conv_s, wkv_s, shift_s)
```

```python
import functools

import jax
import jax.numpy as jnp
from jax import lax
from jax.experimental import pallas as pl
from jax.experimental.pallas import tpu as pltpu

F32 = jnp.float32
BF16 = jnp.bfloat16

EPS = 1e-6
GN_EPS = 64e-5
M_HEADS = 4
CONV_W = 4
R_HEAD = 64
CHUNK = 64
LANES = 128
SUBLANES = 8
VMEM_LIMIT = 48 * 1024 * 1024

GATE_BLOCK = 256


def _params(*sem):
    return pltpu.CompilerParams(dimension_semantics=sem, vmem_limit_bytes=VMEM_LIMIT)


def _rms(x, g):
    return x * lax.rsqrt(jnp.mean(x * x, axis=-1, keepdims=True) + EPS) * g


def _log_sigmoid(x):
    return jnp.minimum(x, 0.0) - jnp.log1p(jnp.exp(-jnp.abs(x)))


def _norm_mm_kernel(x_ref, g_ref, w_ref, o_ref, u_ref):
    @pl.when(pl.program_id(1) == 0)
    def _():
        u_ref[...] = _rms(x_ref[...], g_ref[...]).astype(BF16)

    o_ref[...] = jnp.dot(u_ref[...], w_ref[...], preferred_element_type=F32)


def norm_matmul(x, g, w, tm, tn):
    M, K = x.shape
    N = w.shape[1]
    return pl.pallas_call(
        _norm_mm_kernel,
        grid=(M // tm, N // tn),
        in_specs=[pl.BlockSpec((tm, K), lambda i, j: (i, 0)),
                  pl.BlockSpec((1, K), lambda i, j: (0, 0)),
                  pl.BlockSpec((K, tn), lambda i, j: (0, j))],
        out_specs=pl.BlockSpec((tm, tn), lambda i, j: (i, j)),
        out_shape=jax.ShapeDtypeStruct((M, N), F32),
        scratch_shapes=[pltpu.VMEM((tm, K), BF16)],
        compiler_params=_params("parallel", "arbitrary"),
        name="norm_matmul",
    )(x, g, w)


def _mm_kernel(a_ref, w_ref, o_ref):
    o_ref[...] = jnp.dot(a_ref[...].astype(BF16), w_ref[...], preferred_element_type=F32)


def matmul(a, w, tm):
    M, K = a.shape
    N = w.shape[1]
    return pl.pallas_call(
        _mm_kernel,
        grid=(M // tm,),
        in_specs=[pl.BlockSpec((tm, K), lambda i: (i, 0)),
                  pl.BlockSpec((K, N), lambda i: (0, 0))],
        out_specs=pl.BlockSpec((tm, N), lambda i: (i, 0)),
        out_shape=jax.ShapeDtypeStruct((M, N), F32),
        compiler_params=_params("parallel"),
        name="matmul",
    )(a, w)


def _merge_kernel(h_ref, ga_ref, gb_ref, ya_ref, yb_ref, wa_ref, wb_ref, o_ref):
    ma = (jax.nn.sigmoid(ga_ref[...]) * ya_ref[...]).astype(BF16)
    mb = (jax.nn.sigmoid(gb_ref[...]) * yb_ref[...]).astype(BF16)
    o_ref[...] = (h_ref[...] + jnp.dot(ma, wa_ref[...], preferred_element_type=F32)
                  + jnp.dot(mb, wb_ref[...], preferred_element_type=F32))


def merge_project(h, z, ya, yb, wa, wb, tm):
    M, D = h.shape
    row = pl.BlockSpec((tm, D), lambda i: (i, 0))
    wspec = pl.BlockSpec((D, D), lambda i: (0, 0), pipeline_mode=pl.Buffered(1))
    return pl.pallas_call(
        _merge_kernel,
        grid=(M // tm,),
        in_specs=[row,
                  pl.BlockSpec((tm, D), lambda i: (i, 6)),
                  pl.BlockSpec((tm, D), lambda i: (i, 7)),
                  row, row, wspec, wspec],
        out_specs=row,
        out_shape=jax.ShapeDtypeStruct((M, D), F32),
        compiler_params=_params("parallel"),
        name="merge_project",
    )(h, z, z, ya, yb, wa, wb)


def _ffn_kernel(h_ref, g_ref, wg_ref, wu_ref, wd_ref, o_ref, u_ref, acc_ref):
    f = pl.program_id(1)

    @pl.when(f == 0)
    def _():
        u_ref[...] = _rms(h_ref[...], g_ref[...]).astype(BF16)
        acc_ref[...] = jnp.zeros_like(acc_ref)

    u = u_ref[...]
    gate = jnp.dot(u, wg_ref[...], preferred_element_type=F32)
    up = jnp.dot(u, wu_ref[...], preferred_element_type=F32)
    act = (gate * jax.nn.sigmoid(gate) * up).astype(BF16)
    acc_ref[...] += jnp.dot(act, wd_ref[...], preferred_element_type=F32)

    @pl.when(f == pl.num_programs(1) - 1)
    def _():
        o_ref[...] = h_ref[...] + acc_ref[...]


def ffn(h, g, wg, wu, wd, tm, tf):
    M, D = h.shape
    F = wg.shape[1]
    return pl.pallas_call(
        _ffn_kernel,
        grid=(M // tm, F // tf),
        in_specs=[pl.BlockSpec((tm, D), lambda i, j: (i, 0)),
                  pl.BlockSpec((1, D), lambda i, j: (0, 0)),
                  pl.BlockSpec((D, tf), lambda i, j: (0, j)),
                  pl.BlockSpec((D, tf), lambda i, j: (0, j)),
                  pl.BlockSpec((tf, D), lambda i, j: (j, 0))],
        out_specs=pl.BlockSpec((tm, D), lambda i, j: (i, 0)),
        out_shape=jax.ShapeDtypeStruct((M, D), F32),
        scratch_shapes=[pltpu.VMEM((tm, D), BF16), pltpu.VMEM((tm, D), F32)],
        compiler_params=_params("parallel", "arbitrary"),
        name="ffn",
    )(h, g, wg, wu, wd)


def _ple_kernel(h_ref, g_ref, p_ref, wpg_ref, wpi_ref, gf_ref, o_ref, *, final):
    h = h_ref[...]
    u = _rms(h, g_ref[...]).astype(BF16)
    gate = jnp.dot(u, wpg_ref[...], preferred_element_type=F32)
    emb = jnp.dot(p_ref[...].astype(BF16), wpi_ref[...], preferred_element_type=F32)
    out = h + jax.nn.sigmoid(gate) * emb
    if final:
        out = _rms(out, gf_ref[...])
    o_ref[...] = out


def ple(h, g, p, wpg, wpi, g_final, tm, final):
    M, D = h.shape
    P = p.shape[1]
    return pl.pallas_call(
        functools.partial(_ple_kernel, final=final),
        grid=(M // tm,),
        in_specs=[pl.BlockSpec((tm, D), lambda i: (i, 0)),
                  pl.BlockSpec((1, D), lambda i: (0, 0)),
                  pl.BlockSpec((tm, P), lambda i: (i, 0)),
                  pl.BlockSpec((D, D), lambda i: (0, 0), pipeline_mode=pl.Buffered(1)),
                  pl.BlockSpec((P, D), lambda i: (0, 0), pipeline_mode=pl.Buffered(1)),
                  pl.BlockSpec((1, D), lambda i: (0, 0))],
        out_specs=pl.BlockSpec((tm, D), lambda i: (i, 0)),
        out_shape=jax.ShapeDtypeStruct((M, D), F32),
        compiler_params=_params("parallel"),
        name="ple",
    )(h, g, p, wpg, wpi, g_final)


def _mlstm_kernel(qk_ref, v_ref, o_ref, gt_ref, conv0_ref, cw_ref, cb_ref, gbias_ref, mg_ref,
                  c0_ref, n0_ref, m0_ref,
                  y_ref, convn_ref, c1_ref, n1_ref, m1_ref,
                  xp, C, n, m, *, gate_off):
    c = pl.program_id(1)
    L, D = qk_ref.shape
    H = M_HEADS
    DV = D // H
    DK = DV // 2
    QK = H * DK
    halo = CONV_W - 1
    top = SUBLANES - halo

    @pl.when(c == 0)
    def _():
        xp[top:SUBLANES, :] = conv0_ref[...]
        C[...] = c0_ref[...]
        n[...] = n0_ref[...]
        m[...] = m0_ref[...]

    xp[SUBLANES:SUBLANES + L, :] = qk_ref[...]
    gt = gt_ref[...] + gbias_ref[...]

    row = lax.broadcasted_iota(jnp.int32, (L, L), 0)
    col = lax.broadcasted_iota(jnp.int32, (L, L), 1)
    tril = row >= col
    eye = row == col

    def conv_silu(c0):
        acc = cb_ref[:, c0:c0 + DK] + xp[top:top + L, c0:c0 + DK] * cw_ref[0:1, c0:c0 + DK]
        for j in range(1, CONV_W):
            acc = acc + xp[top + j:top + j + L, c0:c0 + DK] * cw_ref[j:j + 1, c0:c0 + DK]
        return acc * jax.nn.sigmoid(acc)

    for h in range(H):
        q = conv_silu(h * DK)
        k = conv_silu(QK + h * DK) * (DK ** -0.5)
        vb = v_ref[:, h * DV:(h + 1) * DV].astype(BF16)
        li_c = gt[:, gate_off + h:gate_off + h + 1]
        lf_c = _log_sigmoid(gt[:, gate_off + H + h:gate_off + H + h + 1])
        li_r = jnp.sum(jnp.where(eye, li_c, 0.0), axis=0, keepdims=True)
        b_row = jnp.sum(jnp.where(row <= col, lf_c, 0.0), axis=0, keepdims=True)
        b_col = jnp.sum(jnp.where(eye, b_row, 0.0), axis=1, keepdims=True)
        m_old = m[h]
        a_col = b_col + m_old
        dmat = jnp.where(tril, b_col - b_row + li_r, -jnp.inf)
        mt = jnp.maximum(a_col, jnp.max(dmat, axis=1, keepdims=True))
        ea = jnp.exp(a_col - mt)
        p = jnp.exp(dmat - mt)

        qb = q.astype(BF16)
        Ch = C[h]
        nh = n[h]
        s = lax.dot_general(qb, k.astype(BF16), (((1,), (1,)), ((), ())),
                            preferred_element_type=F32) * p
        num = (ea * jnp.dot(qb, Ch.astype(BF16), preferred_element_type=F32)
               + jnp.dot(s.astype(BF16), vb, preferred_element_type=F32))
        den = ea * jnp.sum(q * nh, axis=1, keepdims=True) + jnp.sum(s, axis=1, keepdims=True)
        hh = num / jnp.maximum(jnp.abs(den), jnp.exp(-mt))
        hh = hh * lax.rsqrt(jnp.mean(hh * hh, axis=1, keepdims=True) + EPS)
        y_ref[:, h * DV:(h + 1) * DV] = (jax.nn.sigmoid(o_ref[:, h * DV:(h + 1) * DV])
                                         * (hh * mg_ref[:, h * DV:(h + 1) * DV]))

        m_new = mt[L - 1:L, :]
        dec = jnp.exp(a_col[L - 1:L, :] - m_new)
        wk = jnp.exp(b_col[L - 1:L, :] - b_col + li_c - m_new)
        kw = k * wk
        C[h] = dec * Ch + lax.dot_general(kw.astype(BF16), vb, (((0,), (0,)), ((), ())),
                                          preferred_element_type=F32)
        n[h] = dec * nh + jnp.sum(kw, axis=0, keepdims=True)
        m[h] = m_new

    xp[top:SUBLANES, :] = xp[top + L:SUBLANES + L, :]

    @pl.when(c == pl.num_programs(1) - 1)
    def _():
        convn_ref[...] = xp[top:SUBLANES, :]
        c1_ref[...] = C[...]
        n1_ref[...] = n[...]
        m1_ref[...] = m[...]


def mlstm_branch(z, conv0, C0, n0, m0, conv_w, conv_b, gate_bias, m_norm_g, L, gate_off):
    B, T, _ = z.shape
    H = M_HEADS
    _, _, DK, DV = C0.shape
    D = H * DV
    nc = T // L
    halo = CONV_W - 1

    def seq(j):
        return pl.BlockSpec((None, L, D), lambda b, c: (b, c, j))

    def per_batch(*shape):
        return pl.BlockSpec((None,) + shape, lambda b, c: (b,) + (0,) * len(shape))

    def const(*shape):
        return pl.BlockSpec(shape, lambda b, c: (0,) * len(shape))

    gate_blk = (8 * D + GATE_BLOCK) // GATE_BLOCK
    y, conv_new, C1, n1, m1 = pl.pallas_call(
        functools.partial(_mlstm_kernel, gate_off=gate_off),
        grid=(B, nc),
        in_specs=[seq(0), seq(1), seq(2),
                  pl.BlockSpec((None, L, GATE_BLOCK), lambda b, c: (b, c, gate_blk)),
                  per_batch(halo, D), const(CONV_W, D), const(1, D), const(1, GATE_BLOCK),
                  const(1, D),
                  per_batch(H, DK, DV), per_batch(H, 1, DK), per_batch(H, 1, 1)],
        out_specs=[seq(0), per_batch(halo, D),
                   per_batch(H, DK, DV), per_batch(H, 1, DK), per_batch(H, 1, 1)],
        out_shape=[jax.ShapeDtypeStruct((B, T, D), F32),
                   jax.ShapeDtypeStruct((B, halo, D), F32),
                   jax.ShapeDtypeStruct((B, H, DK, DV), F32),
                   jax.ShapeDtypeStruct((B, H, 1, DK), F32),
                   jax.ShapeDtypeStruct((B, H, 1, 1), F32)],
        scratch_shapes=[pltpu.VMEM((L + SUBLANES, D), F32), pltpu.VMEM((H, DK, DV), F32),
                        pltpu.VMEM((H, 1, DK), F32), pltpu.VMEM((H, 1, 1), F32)],
        compiler_params=_params("parallel", "arbitrary"),
        name="mlstm",
    )(z, z, z, z, conv0, conv_w, conv_b, gate_bias, m_norm_g,
      C0, n0[:, :, None, :], m0[:, :, None, None])
    return y, conv_new, C1, n1[:, :, 0, :], m1[:, :, 0, 0]


RW_GROUPS = 4
RW_BATCH = 4


def _group_sum(x, rh):
    shift = rh
    while shift < LANES:
        x = x + pltpu.roll(x, shift, axis=1)
        shift *= 2
    return x


def _rwkv_kernel(zr_ref, zg_ref, zs_ref, sh_r_ref, sh_g_ref, sh_s_ref,
                 mu_r_ref, mu_g_ref, mu_s_ref, ww2_ref, wa2_ref, wg2_ref,
                 w0_ref, a0_ref, kkw_ref, ka_ref, rk_ref, gnw_ref, gnb_ref, s0_ref,
                 y_ref, shn_r_ref, shn_g_ref, shn_s_ref, s1_ref,
                 S, car_r, car_g, car_s, R, W, K, V, KK, BT, G, O):
    c = pl.program_id(1)
    NB, Tc, D3 = zr_ref.shape
    D = D3 // 3
    RH = D // R_HEAD
    NT = D // LANES
    VH = R_HEAD // SUBLANES

    @pl.when(c == 0)
    def _():
        S[...] = s0_ref[...]
        car_r[...] = sh_r_ref[...]
        car_g[...] = sh_g_ref[...]
        car_s[...] = sh_s_ref[...]

    first_row = lax.broadcasted_iota(jnp.int32, (Tc, LANES), 0) == 0

    def shifted(x, carry_row, mu):
        prev = jnp.where(first_row, carry_row, pltpu.roll(x, 1, axis=0))
        return x + mu * (prev - x)

    def tile(j):
        return slice(j * LANES, (j + 1) * LANES)

    for bi in range(NB):
        xs = [shifted(zs_ref[bi, :, tile(j)], car_s[bi, :, tile(j)], mu_s_ref[:, tile(j)])
              for j in range(GATE_BLOCK // LANES)]
        xg = [shifted(zg_ref[bi, :, tile(j)], car_g[bi, :, tile(j)], mu_g_ref[:, tile(j)])
              for j in range(GATE_BLOCK // LANES)]
        s_small = jnp.concatenate(xs, axis=1)
        s_gate = jnp.concatenate(xg, axis=1)
        lw = jnp.dot(jnp.tanh(s_small).astype(BF16), ww2_ref[...], preferred_element_type=F32)
        la = jnp.dot(s_small.astype(BF16), wa2_ref[...], preferred_element_type=F32)
        lg = jnp.dot(jax.nn.sigmoid(s_gate).astype(BF16), wg2_ref[...],
                     preferred_element_type=F32)
        ss = jnp.zeros((Tc, LANES), F32)
        for j in range(NT):
            cj = tile(j)
            r, k, v = (shifted(zr_ref[bi, :, tile(b * NT + j)], car_r[bi, :, tile(b * NT + j)],
                               mu_r_ref[:, tile(b * NT + j)]) for b in range(3))
            a = jax.nn.sigmoid(a0_ref[:, cj] + la[:, cj])
            kk = k * kkw_ref[:, cj]
            ss = ss + kk * kk
            R[bi, j] = r
            V[bi, j] = v
            KK[bi, j] = kk
            K[bi, j] = k * (1.0 + (a - 1.0) * ka_ref[:, cj])
            BT[bi, j] = a
            W[bi, j] = jnp.exp(-jnp.exp(_log_sigmoid(w0_ref[:, cj] + lw[:, cj]) - 0.5))
            G[bi, j] = lg[:, cj]
        inv = 1.0 / jnp.maximum(jnp.sqrt(_group_sum(ss, RH)), 1e-12)
        for j in range(NT):
            kk = KK[bi, j] * inv
            KK[bi, j] = kk
            BT[bi, j] = kk * BT[bi, j]
        car_r[bi] = zr_ref[bi, Tc - 1:Tc, :]
        car_g[bi] = zg_ref[bi, Tc - 1:Tc, :]
        car_s[bi] = zs_ref[bi, Tc - 1:Tc, :]

    sub = lax.broadcasted_iota(jnp.int32, (SUBLANES, LANES), 0)
    grp = lax.broadcasted_iota(jnp.int32, (SUBLANES, LANES), 1) // RH
    low_half = sub < RW_GROUPS
    own_group = grp == sub % RW_GROUPS

    def step(t, carry):
        row = pl.ds(t, 1)
        for bi in range(NB):
            kk_rows = [KK[bi, j, row, :] for j in range(NT)]
            sa = []
            for vh in range(VH):
                acc = S[bi, vh, 0] * kk_rows[0]
                for kh in range(1, NT):
                    acc = acc + S[bi, vh, kh] * kk_rows[kh]
                sa.append(-_group_sum(acc, RH))
            vrep = []
            for vh in range(VH):
                lo = jnp.broadcast_to(V[bi, 2 * vh, row, :], (SUBLANES, LANES))
                hi = jnp.broadcast_to(V[bi, 2 * vh + 1, row, :], (SUBLANES, LANES))
                picked = jnp.where(own_group, jnp.where(low_half, lo, hi), 0.0)
                vrep.append(_group_sum(picked, RH))
            oacc = [None] * VH
            for kh in range(NT):
                w_row = W[bi, kh, row, :]
                b_row = BT[bi, kh, row, :]
                k_row = K[bi, kh, row, :]
                r_row = R[bi, kh, row, :]
                for vh in range(VH):
                    s_new = S[bi, vh, kh] * w_row + sa[vh] * b_row + vrep[vh] * k_row
                    S[bi, vh, kh] = s_new
                    term = s_new * r_row
                    oacc[vh] = term if oacc[vh] is None else oacc[vh] + term
            for vh in range(VH):
                tot = jnp.where(own_group, _group_sum(oacc[vh], RH), 0.0)
                tot = tot + pltpu.roll(tot, SUBLANES - 1, axis=0)
                tot = tot + pltpu.roll(tot, SUBLANES - 2, axis=0)
                O[bi, 2 * vh, row, :] = tot[0:1, :]
                O[bi, 2 * vh + 1, row, :] = tot[RW_GROUPS:RW_GROUPS + 1, :]
        return carry

    lax.fori_loop(0, Tc, step, 0)

    inv_n = 1.0 / R_HEAD
    for bi in range(NB):
        tot = jnp.zeros((Tc, LANES), F32)
        bon = jnp.zeros((Tc, LANES), F32)
        for j in range(NT):
            tot = tot + O[bi, j]
            bon = bon + R[bi, j] * K[bi, j] * rk_ref[:, tile(j)]
        mean = _group_sum(tot, RH) * inv_n
        bon = _group_sum(bon, RH)
        var = jnp.zeros((Tc, LANES), F32)
        for j in range(NT):
            d = O[bi, j] - mean
            var = var + d * d
        rstd = lax.rsqrt(_group_sum(var, RH) * inv_n + GN_EPS)
        for j in range(NT):
            cj = tile(j)
            o = (O[bi, j] - mean) * rstd * gnw_ref[:, cj] + gnb_ref[:, cj]
            y_ref[bi, :, cj] = (o + bon * V[bi, j]) * G[bi, j]

    @pl.when(c == pl.num_programs(1) - 1)
    def _():
        s1_ref[...] = S[...]
        shn_r_ref[...] = car_r[...]
        shn_g_ref[...] = car_g[...]
        shn_s_ref[...] = car_s[...]


def rwkv_branch(z, sh_r, sh_g, sh_s, S0, P, steps):
    B, T, _ = z.shape
    D = sh_r.shape[-1] // 3
    NB = RW_BATCH
    NT = D // LANES
    VH = R_HEAD // SUBLANES
    assert D // R_HEAD * RW_GROUPS == LANES and B % NB == 0 and T % steps == 0
    g_blk = 8 * D // GATE_BLOCK

    def seq(width, j):
        return pl.BlockSpec((NB, steps, width), lambda g, c: (g, c, j))

    def per_batch(*shape):
        return pl.BlockSpec((NB,) + shape, lambda g, c: (g,) + (0,) * len(shape))

    def const(*shape):
        return pl.BlockSpec(shape, lambda g, c: (0,) * len(shape))

    act = pltpu.VMEM((NB, NT, steps, LANES), F32)
    y, shn_r, shn_g, shn_s, S1 = pl.pallas_call(
        _rwkv_kernel,
        grid=(B // NB, T // steps),
        in_specs=[seq(3 * D, 1), seq(GATE_BLOCK, g_blk), seq(GATE_BLOCK, g_blk + 1),
                  per_batch(1, 3 * D), per_batch(1, GATE_BLOCK), per_batch(1, GATE_BLOCK),
                  const(1, 3 * D), const(1, GATE_BLOCK), const(1, GATE_BLOCK),
                  const(GATE_BLOCK, D), const(GATE_BLOCK, D), const(GATE_BLOCK, D)]
                 + [const(1, D)] * 7
                 + [per_batch(VH, NT, SUBLANES, LANES)],
        out_specs=[seq(D, 0), per_batch(1, 3 * D), per_batch(1, GATE_BLOCK),
                   per_batch(1, GATE_BLOCK), per_batch(VH, NT, SUBLANES, LANES)],
        out_shape=[jax.ShapeDtypeStruct((B, T, D), F32),
                   jax.ShapeDtypeStruct((B, 1, 3 * D), F32),
                   jax.ShapeDtypeStruct((B, 1, GATE_BLOCK), F32),
                   jax.ShapeDtypeStruct((B, 1, GATE_BLOCK), F32),
                   jax.ShapeDtypeStruct((B, VH, NT, SUBLANES, LANES), F32)],
        scratch_shapes=[pltpu.VMEM((NB, VH, NT, SUBLANES, LANES), F32),
                        pltpu.VMEM((NB, 1, 3 * D), F32), pltpu.VMEM((NB, 1, GATE_BLOCK), F32),
                        pltpu.VMEM((NB, 1, GATE_BLOCK), F32)] + [act] * 8,
        compiler_params=_params("parallel", "arbitrary"),
        name="rwkv7",
    )(z, z, z, sh_r, sh_g, sh_s, P["mu_r"], P["mu_g"], P["mu_s"],
      P["w_w2"], P["w_a2"], P["w_g2"], P["w0"], P["a0"], P["k_k"], P["k_a"], P["r_k"],
      P["gn_w"], P["gn_b"], S0)
    return y, shn_r, shn_g, shn_s, S1


def _nh(x):
    lead = x.shape[:-1]
    return x.reshape(lead + (-1, R_HEAD)).swapaxes(-1, -2).reshape(x.shape)


def _hn(x):
    lead = x.shape[:-1]
    return x.reshape(lead + (R_HEAD, -1)).swapaxes(-1, -2).reshape(x.shape)


def _tile_state(S):
    B, RH = S.shape[:2]
    G = RW_GROUPS
    S = S.reshape(B, RH, R_HEAD // SUBLANES, SUBLANES, R_HEAD // G, G)
    return S.transpose(0, 2, 4, 3, 5, 1).reshape(B, R_HEAD // SUBLANES, R_HEAD // G, SUBLANES, G * RH)


def _untile_state(S):
    B, VH, NT = S.shape[:3]
    G = RW_GROUPS
    RH = S.shape[-1] // G
    S = S.reshape(B, VH, NT, SUBLANES, G, RH)
    return S.transpose(0, 5, 1, 3, 2, 4).reshape(B, RH, R_HEAD, R_HEAD)


def _pick_tile(n, candidates):
    for c in candidates:
        if n % c == 0:
            return c
    return n


def _repack_w_in(w, D, lora_w, lora_a, lora_g):
    H = M_HEADS
    o_end = 3 * D
    f_end = o_end + 2 * H
    zr = f_end
    xw = zr + 3 * D
    xa = xw + lora_w
    xg = xa + lora_a
    ga = xg + lora_g
    small = lora_w + lora_a + 2 * H
    assert lora_g == GATE_BLOCK and small <= GATE_BLOCK
    pad = jnp.zeros((w.shape[0], GATE_BLOCK - small), w.dtype)
    rkv = [_nh(w[:, zr + i * D:zr + (i + 1) * D]) for i in range(3)]
    return jnp.concatenate(
        [w[:, :o_end]] + rkv + [w[:, ga:ga + D], _nh(w[:, ga + D:ga + 2 * D]), w[:, xg:ga],
                                w[:, xw:xg], w[:, o_end:f_end], pad], axis=1)


def _layer(h, p_l, conv0, C0, n0, m0, shift0, S0, chunk, W, final, g_final):
    B, T, D = h.shape
    M = B * T
    tm = min(512, M)
    H = M_HEADS
    RH = D // R_HEAD
    lora_w, lora_a, lora_g = W["lora"]
    gate_off = lora_w + lora_a

    h2 = h.reshape(M, D)
    z2 = norm_matmul(h2, W["g_mix"], W["w_in"], min(1024, M), W["w_in_tn"])
    z = z2.reshape(B, T, -1)

    y_a, conv_new, C1, n1, m1 = mlstm_branch(z, conv0, C0, n0, m0, W["conv_w"], W["conv_b"],
                                             W["gate_bias"], W["m_norm_g"], chunk, gate_off)

    sh_r = jnp.concatenate([_nh(shift0[..., i * D:(i + 1) * D]) for i in range(3)], axis=-1)
    sh_g = shift0[..., 3 * D + gate_off:]
    sh_s = jnp.pad(shift0[..., 3 * D:3 * D + gate_off], ((0, 0), (0, 0), (0, GATE_BLOCK - gate_off)))
    y_b, shn_r, shn_g, shn_s, S1 = rwkv_branch(z, sh_r, sh_g, sh_s, _tile_state(S0), W["rwkv"],
                                               min(32, T))
    shift_new = jnp.concatenate([_hn(shn_r[..., i * D:(i + 1) * D]) for i in range(3)]
                                + [shn_s[..., :gate_off], shn_g], axis=-1)
    S1 = _untile_state(S1)

    h2 = merge_project(h2, z2, y_a.reshape(M, D), y_b.reshape(M, D), W["w_out"], W["w_out_nh"],
                       min(256, M))
    h2 = ffn(h2, W["g_ffn"], W["w_gate"], W["w_up"], W["w_down"], tm, W["ffn_tf"])
    h2 = ple(h2, W["g_ple"], p_l.reshape(M, -1), W["w_ple_gate"], W["w_ple_in"], g_final, tm, final)
    return h2.reshape(B, T, D), (conv_new, C1, n1, m1, shift_new, S1)


def kernel(x_prompt, x_sample, p_prompt, p_sample, state_mlstm_C, state_mlstm_n, state_mlstm_m, state_mlstm_conv, state_rwkv_wkv, state_rwkv_shift, g_mix, w_in, conv_w, conv_b, b_i, b_f, m_norm_g, rwkv_mu, w0, w_w2, a0, w_a2, w_g2, k_k, k_a, r_k, gn_w, gn_b, w_out, g_ffn, w_gate, w_up, w_down, g_ple, w_ple_gate, w_ple_in, g_final):
    depth = w_in.shape[0]
    D = x_prompt.shape[-1]
    H = M_HEADS
    DV = D // H
    DK = DV // 2
    QK = H * DK
    RH = D // R_HEAD
    r_shift = state_rwkv_shift.shape[-1]
    lora = (w_w2.shape[1], w_a2.shape[1], w_g2.shape[1])
    n_pack = 8 * D + 2 * GATE_BLOCK
    w_in_tn = _pick_tile(n_pack, (768, 512, 256, 128))

    layers = []
    for l in range(depth):
        gate_bias = jnp.zeros((1, GATE_BLOCK), F32)
        gate_bias = gate_bias.at[0, lora[0] + lora[1]:lora[0] + lora[1] + 2 * H].set(
            jnp.concatenate([b_i[l], b_f[l]]))
        mu = rwkv_mu[l]
        lw, la = lora[0], lora[1]
        w_w2p = jnp.zeros((GATE_BLOCK, D), F32).at[:lw].set(_nh(w_w2[l]))
        w_a2p = jnp.zeros((GATE_BLOCK, D), F32).at[lw:lw + la].set(_nh(w_a2[l]))
        rwkv = dict(
            mu_r=jnp.concatenate([_nh(mu[i * D:(i + 1) * D]) for i in range(3)])[None],
            mu_g=mu[3 * D + lw + la:][None],
            mu_s=jnp.pad(mu[3 * D:3 * D + lw + la], (0, GATE_BLOCK - lw - la))[None],
            w_w2=w_w2p.astype(BF16), w_a2=w_a2p.astype(BF16), w_g2=_nh(w_g2[l]).astype(BF16),
            w0=_nh(w0[l])[None], a0=_nh(a0[l])[None], k_k=_nh(k_k[l])[None],
            k_a=_nh(k_a[l])[None], r_k=_nh(r_k[l].reshape(-1))[None],
            gn_w=_nh(gn_w[l])[None], gn_b=_nh(gn_b[l])[None])
        layers.append(dict(
            g_mix=g_mix[l][None], w_in=_repack_w_in(w_in[l], D, *lora).astype(BF16),
            w_in_tn=w_in_tn, lora=lora,
            conv_w=conv_w[l], conv_b=conv_b[l][None], gate_bias=gate_bias,
            m_norm_g=m_norm_g[l][None], rwkv=rwkv,
            w_out=w_out[l].astype(BF16),
            w_out_nh=_nh(w_out[l].T).T.astype(BF16), g_ffn=g_ffn[l][None],
            w_gate=w_gate[l].astype(BF16), w_up=w_up[l].astype(BF16),
            w_down=w_down[l].astype(BF16), ffn_tf=_pick_tile(w_gate.shape[2], (512, 256, 128)),
            g_ple=g_ple[l][None], w_ple_gate=w_ple_gate[l].astype(BF16),
            w_ple_in=w_ple_in[l].astype(BF16)))
    gf = g_final[None]

    def run(x, p, conv0, C0, n0, m0, shift0, S0, chunk):
        h = x
        outs = ([], [], [], [], [], [])
        for l in range(depth):
            h, st = _layer(h, p[l], conv0[l], C0[l], n0[l], m0[l], shift0[l], S0[l], chunk,
                           layers[l], l == depth - 1, gf)
            for lst, s in zip(outs, st):
                lst.append(s)
        return h, [jnp.stack(o, axis=0) for o in outs]

    Bp = x_prompt.shape[0]
    y_prompt, (conv_p, C_p, n_p, m_p, shift_p, wkv_p) = run(
        x_prompt, p_prompt,
        jnp.zeros((depth, Bp, CONV_W - 1, 2 * QK), F32),
        jnp.zeros((depth, Bp, H, DK, DV), F32),
        jnp.zeros((depth, Bp, H, DK), F32),
        jnp.zeros((depth, Bp, H), F32),
        jnp.zeros((depth, Bp, 1, r_shift), F32),
        jnp.zeros((depth, Bp, RH, R_HEAD, R_HEAD), F32),
        CHUNK)
    y_sample, (conv_s, C_s, n_s, m_s, shift_s, wkv_s) = run(
        x_sample, p_sample, state_mlstm_conv, state_mlstm_C, state_mlstm_n, state_mlstm_m,
        state_rwkv_shift, state_rwkv_wkv, x_sample.shape[1])
    return (y_prompt, y_sample, C_p, n_p, m_p, conv_p, wkv_p, shift_p,
            C_s, n_s, m_s, conv_s, wkv_s, shift_s)
```

```python
import functools

import jax
import jax.numpy as jnp
from jax import lax
from jax.experimental import pallas as pl
from jax.experimental.pallas import tpu as pltpu

F32 = jnp.float32
BF16 = jnp.bfloat16

EPS = 1e-6
GN_EPS = 64e-5
M_HEADS = 4
CONV_W = 4
R_HEAD = 64
CHUNK = 64
LANES = 128
SUBLANES = 8
VMEM_LIMIT = 48 * 1024 * 1024

GATE_BLOCK = 256


def _params(*sem):
    return pltpu.CompilerParams(dimension_semantics=sem, vmem_limit_bytes=VMEM_LIMIT)


def _rms(x, g):
    return x * lax.rsqrt(jnp.mean(x * x, axis=-1, keepdims=True) + EPS) * g


def _log_sigmoid(x):
    return jnp.minimum(x, 0.0) - jnp.log1p(jnp.exp(-jnp.abs(x)))


def _norm_mm_kernel(x_ref, g_ref, w_ref, o_ref, u_ref):
    @pl.when(pl.program_id(1) == 0)
    def _():
        u_ref[...] = _rms(x_ref[...], g_ref[...]).astype(BF16)

    o_ref[...] = jnp.dot(u_ref[...], w_ref[...], preferred_element_type=F32)


def norm_matmul(x, g, w, tm, tn):
    M, K = x.shape
    N = w.shape[1]
    return pl.pallas_call(
        _norm_mm_kernel,
        grid=(M // tm, N // tn),
        in_specs=[pl.BlockSpec((tm, K), lambda i, j: (i, 0)),
                  pl.BlockSpec((1, K), lambda i, j: (0, 0)),
                  pl.BlockSpec((K, tn), lambda i, j: (0, j))],
        out_specs=pl.BlockSpec((tm, tn), lambda i, j: (i, j)),
        out_shape=jax.ShapeDtypeStruct((M, N), F32),
        scratch_shapes=[pltpu.VMEM((tm, K), BF16)],
        compiler_params=_params("parallel", "arbitrary"),
        name="norm_matmul",
    )(x, g, w)


def _merge_kernel(h_ref, ga_ref, gb_ref, ya_ref, yb_ref, wa_ref, wb_ref, o_ref):
    ma = (jax.nn.sigmoid(ga_ref[...]) * ya_ref[...]).astype(BF16)
    mb = (jax.nn.sigmoid(gb_ref[...]) * yb_ref[...]).astype(BF16)
    o_ref[...] = (h_ref[...] + jnp.dot(ma, wa_ref[...], preferred_element_type=F32)
                  + jnp.dot(mb, wb_ref[...], preferred_element_type=F32))


def merge_project(h, z, ya, yb, wa, wb, tm):
    M, D = h.shape
    row = pl.BlockSpec((tm, D), lambda i: (i, 0))
    wspec = pl.BlockSpec((D, D), lambda i: (0, 0), pipeline_mode=pl.Buffered(1))
    return pl.pallas_call(
        _merge_kernel,
        grid=(M // tm,),
        in_specs=[row,
                  pl.BlockSpec((tm, D), lambda i: (i, 6)),
                  pl.BlockSpec((tm, D), lambda i: (i, 7)),
                  row, row, wspec, wspec],
        out_specs=row,
        out_shape=jax.ShapeDtypeStruct((M, D), F32),
        compiler_params=_params("parallel"),
        name="merge_project",
    )(h, z, z, ya, yb, wa, wb)


def _ffn_kernel(h_ref, g_ref, wg_ref, wu_ref, wd_ref, o_ref, u_ref, acc_ref):
    f = pl.program_id(1)

    @pl.when(f == 0)
    def _():
        u_ref[...] = _rms(h_ref[...], g_ref[...]).astype(BF16)
        acc_ref[...] = jnp.zeros_like(acc_ref)

    u = u_ref[...]
    gate = jnp.dot(u, wg_ref[...], preferred_element_type=F32)
    up = jnp.dot(u, wu_ref[...], preferred_element_type=F32)
    act = (gate * jax.nn.sigmoid(gate) * up).astype(BF16)
    acc_ref[...] += jnp.dot(act, wd_ref[...], preferred_element_type=F32)

    @pl.when(f == pl.num_programs(1) - 1)
    def _():
        o_ref[...] = h_ref[...] + acc_ref[...]


def ffn(h, g, wg, wu, wd, tm, tf):
    M, D = h.shape
    F = wg.shape[1]
    return pl.pallas_call(
        _ffn_kernel,
        grid=(M // tm, F // tf),
        in_specs=[pl.BlockSpec((tm, D), lambda i, j: (i, 0)),
                  pl.BlockSpec((1, D), lambda i, j: (0, 0)),
                  pl.BlockSpec((D, tf), lambda i, j: (0, j)),
                  pl.BlockSpec((D, tf), lambda i, j: (0, j)),
                  pl.BlockSpec((tf, D), lambda i, j: (j, 0))],
        out_specs=pl.BlockSpec((tm, D), lambda i, j: (i, 0)),
        out_shape=jax.ShapeDtypeStruct((M, D), F32),
        scratch_shapes=[pltpu.VMEM((tm, D), BF16), pltpu.VMEM((tm, D), F32)],
        compiler_params=_params("parallel", "arbitrary"),
        name="ffn",
    )(h, g, wg, wu, wd)


def _ple_kernel(h_ref, g_ref, p_ref, wpg_ref, wpi_ref, gf_ref, o_ref, *, final):
    h = h_ref[...]
    u = _rms(h, g_ref[...]).astype(BF16)
    gate = jnp.dot(u, wpg_ref[...], preferred_element_type=F32)
    emb = jnp.dot(p_ref[...].astype(BF16), wpi_ref[...], preferred_element_type=F32)
    out = h + jax.nn.sigmoid(gate) * emb
    if final:
        out = _rms(out, gf_ref[...])
    o_ref[...] = out


def ple(h, g, p, wpg, wpi, g_final, tm, final):
    M, D = h.shape
    P = p.shape[1]
    return pl.pallas_call(
        functools.partial(_ple_kernel, final=final),
        grid=(M // tm,),
        in_specs=[pl.BlockSpec((tm, D), lambda i: (i, 0)),
                  pl.BlockSpec((1, D), lambda i: (0, 0)),
                  pl.BlockSpec((tm, P), lambda i: (i, 0)),
                  pl.BlockSpec((D, D), lambda i: (0, 0), pipeline_mode=pl.Buffered(1)),
                  pl.BlockSpec((P, D), lambda i: (0, 0), pipeline_mode=pl.Buffered(1)),
                  pl.BlockSpec((1, D), lambda i: (0, 0))],
        out_specs=pl.BlockSpec((tm, D), lambda i: (i, 0)),
        out_shape=jax.ShapeDtypeStruct((M, D), F32),
        compiler_params=_params("parallel"),
        name="ple",
    )(h, g, p, wpg, wpi, g_final)


def _mlstm_kernel(qk_ref, v_ref, o_ref, gt_ref, conv0_ref, cw_ref, cb_ref, gbias_ref, mg_ref,
                  c0_ref, n0_ref, m0_ref,
                  y_ref, convn_ref, c1_ref, n1_ref, m1_ref,
                  xp, C, n, m, *, gate_off):
    c = pl.program_id(1)
    L, D = qk_ref.shape
    H = M_HEADS
    DV = D // H
    DK = DV // 2
    QK = H * DK
    halo = CONV_W - 1
    top = SUBLANES - halo

    @pl.when(c == 0)
    def _():
        xp[top:SUBLANES, :] = conv0_ref[...]
        C[...] = c0_ref[...]
        n[...] = n0_ref[...]
        m[...] = m0_ref[...]

    xp[SUBLANES:SUBLANES + L, :] = qk_ref[...]
    gt = gt_ref[...] + gbias_ref[...]

    row = lax.broadcasted_iota(jnp.int32, (L, L), 0)
    col = lax.broadcasted_iota(jnp.int32, (L, L), 1)
    tril = row >= col
    eye = row == col

    def conv_silu(c0):
        acc = cb_ref[:, c0:c0 + DK] + xp[top:top + L, c0:c0 + DK] * cw_ref[0:1, c0:c0 + DK]
        for j in range(1, CONV_W):
            acc = acc + xp[top + j:top + j + L, c0:c0 + DK] * cw_ref[j:j + 1, c0:c0 + DK]
        return acc * jax.nn.sigmoid(acc)

    for h in range(H):
        q = conv_silu(h * DK)
        k = conv_silu(QK + h * DK) * (DK ** -0.5)
        vb = v_ref[:, h * DV:(h + 1) * DV].astype(BF16)
        li_c = gt[:, gate_off + h:gate_off + h + 1]
        lf_c = _log_sigmoid(gt[:, gate_off + H + h:gate_off + H + h + 1])
        li_r = jnp.sum(jnp.where(eye, li_c, 0.0), axis=0, keepdims=True)
        b_row = jnp.sum(jnp.where(row <= col, lf_c, 0.0), axis=0, keepdims=True)
        b_col = jnp.sum(jnp.where(eye, b_row, 0.0), axis=1, keepdims=True)
        m_old = m[h]
        a_col = b_col + m_old
        dmat = jnp.where(tril, b_col - b_row + li_r, -jnp.inf)
        mt = jnp.maximum(a_col, jnp.max(dmat, axis=1, keepdims=True))
        ea = jnp.exp(a_col - mt)
        p = jnp.exp(dmat - mt)

        qb = q.astype(BF16)
        Ch = C[h]
        nh = n[h]
        s = lax.dot_general(qb, k.astype(BF16), (((1,), (1,)), ((), ())),
                            preferred_element_type=F32) * p
        num = (ea * jnp.dot(qb, Ch.astype(BF16), preferred_element_type=F32)
               + jnp.dot(s.astype(BF16), vb, preferred_element_type=F32))
        den = ea * jnp.sum(q * nh, axis=1, keepdims=True) + jnp.sum(s, axis=1, keepdims=True)
        hh = num / jnp.maximum(jnp.abs(den), jnp.exp(-mt))
        hh = hh * lax.rsqrt(jnp.mean(hh * hh, axis=1, keepdims=True) + EPS)
        y_ref[:, h * DV:(h + 1) * DV] = (jax.nn.sigmoid(o_ref[:, h * DV:(h + 1) * DV])
                                         * (hh * mg_ref[:, h * DV:(h + 1) * DV]))

        m_new = mt[L - 1:L, :]
        dec = jnp.exp(a_col[L - 1:L, :] - m_new)
        wk = jnp.exp(b_col[L - 1:L, :] - b_col + li_c - m_new)
        kw = k * wk
        C[h] = dec * Ch + lax.dot_general(kw.astype(BF16), vb, (((0,), (0,)), ((), ())),
                                          preferred_element_type=F32)
        n[h] = dec * nh + jnp.sum(kw, axis=0, keepdims=True)
        m[h] = m_new

    xp[top:SUBLANES, :] = xp[top + L:SUBLANES + L, :]

    @pl.when(c == pl.num_programs(1) - 1)
    def _():
        convn_ref[...] = xp[top:SUBLANES, :]
        c1_ref[...] = C[...]
        n1_ref[...] = n[...]
        m1_ref[...] = m[...]


def mlstm_branch(z, conv0, C0, n0, m0, conv_w, conv_b, gate_bias, m_norm_g, L, gate_off):
    B, T, _ = z.shape
    H = M_HEADS
    _, _, DK, DV = C0.shape
    D = H * DV
    nc = T // L
    halo = CONV_W - 1

    def seq(j):
        return pl.BlockSpec((None, L, D), lambda b, c: (b, c, j))

    def per_batch(*shape):
        return pl.BlockSpec((None,) + shape, lambda b, c: (b,) + (0,) * len(shape))

    def const(*shape):
        return pl.BlockSpec(shape, lambda b, c: (0,) * len(shape))

    gate_blk = (8 * D + GATE_BLOCK) // GATE_BLOCK
    y, conv_new, C1, n1, m1 = pl.pallas_call(
        functools.partial(_mlstm_kernel, gate_off=gate_off),
        grid=(B, nc),
        in_specs=[seq(0), seq(1), seq(2),
                  pl.BlockSpec((None, L, GATE_BLOCK), lambda b, c: (b, c, gate_blk)),
                  per_batch(halo, D), const(CONV_W, D), const(1, D), const(1, GATE_BLOCK),
                  const(1, D),
                  per_batch(H, DK, DV), per_batch(H, 1, DK), per_batch(H, 1, 1)],
        out_specs=[seq(0), per_batch(halo, D),
                   per_batch(H, DK, DV), per_batch(H, 1, DK), per_batch(H, 1, 1)],
        out_shape=[jax.ShapeDtypeStruct((B, T, D), F32),
                   jax.ShapeDtypeStruct((B, halo, D), F32),
                   jax.ShapeDtypeStruct((B, H, DK, DV), F32),
                   jax.ShapeDtypeStruct((B, H, 1, DK), F32),
                   jax.ShapeDtypeStruct((B, H, 1, 1), F32)],
        scratch_shapes=[pltpu.VMEM((L + SUBLANES, D), F32), pltpu.VMEM((H, DK, DV), F32),
                        pltpu.VMEM((H, 1, DK), F32), pltpu.VMEM((H, 1, 1), F32)],
        compiler_params=_params("parallel", "arbitrary"),
        name="mlstm",
    )(z, z, z, z, conv0, conv_w, conv_b, gate_bias, m_norm_g,
      C0, n0[:, :, None, :], m0[:, :, None, None])
    return y, conv_new, C1, n1[:, :, 0, :], m1[:, :, 0, 0]


RW_BATCH = 4
DECAY_SCALE = 0.6065306597126334


def _block_transpose(a, rh):
    lane = lax.broadcasted_iota(jnp.int32, a[0].shape, 1)
    low = lane < 2 * rh
    even = (lane // rh) % 2 == 0
    c0 = jnp.where(low, a[0], pltpu.roll(a[2], 2 * rh, axis=1))
    c2 = jnp.where(low, pltpu.roll(a[0], 2 * rh, axis=1), a[2])
    c1 = jnp.where(low, a[1], pltpu.roll(a[3], 2 * rh, axis=1))
    c3 = jnp.where(low, pltpu.roll(a[1], 2 * rh, axis=1), a[3])
    return (jnp.where(even, c0, pltpu.roll(c1, rh, axis=1)),
            jnp.where(even, pltpu.roll(c0, 3 * rh, axis=1), c1),
            jnp.where(even, c2, pltpu.roll(c3, rh, axis=1)),
            jnp.where(even, pltpu.roll(c2, 3 * rh, axis=1), c3))


def _rwkv_kernel(zr_ref, zg_ref, zs_ref, shp_r_ref, sh_g_ref, sh_s_ref,
                 mu_r_ref, mu_g_ref, mu_s_ref, ww2_ref, wa2_ref, wg2_ref,
                 w0_ref, a0_ref, kkw_ref, ka_ref, rk_ref, gnw_ref, gnb_ref, s0_ref,
                 y_ref, shn_r_ref, shn_g_ref, shn_s_ref, s1_ref,
                 S, car_r, car_g, car_s, LW, LA, LG, R, W, K, KK, BT, VP, OP):
    c = pl.program_id(1)
    NB, Tc, D3 = zr_ref.shape
    D = D3 // 3
    RH = D // R_HEAD
    N = R_HEAD
    NT = D // LANES
    VH = N // SUBLANES

    @pl.when(c == 0)
    def _():
        S[...] = s0_ref[...]
        car_r[...] = shp_r_ref[...]
        car_g[...] = sh_g_ref[...]
        car_s[...] = sh_s_ref[...]

    first_row = lax.broadcasted_iota(jnp.int32, (Tc, LANES), 0) == 0

    def shifted(x, carry_row, mu):
        prev = jnp.where(first_row, carry_row, pltpu.roll(x, 1, axis=0))
        return x + mu * (prev - x)

    def tile(j):
        return slice(j * LANES, (j + 1) * LANES)

    def value_rows(n):
        return (n // SUBLANES, pl.ds(n % SUBLANES, Tc, stride=SUBLANES), slice(None))

    for bi in range(NB):
        xs = [shifted(zs_ref[bi, :, tile(j)], car_s[bi, :, tile(j)], mu_s_ref[:, tile(j)])
              for j in range(GATE_BLOCK // LANES)]
        xg = [shifted(zg_ref[bi, :, tile(j)], car_g[bi, :, tile(j)], mu_g_ref[:, tile(j)])
              for j in range(GATE_BLOCK // LANES)]
        s_small = jnp.concatenate(xs, axis=1)
        s_gate = jnp.concatenate(xg, axis=1)
        lw = jnp.dot(jnp.tanh(s_small).astype(BF16), ww2_ref[...], preferred_element_type=F32)
        la = jnp.dot(s_small.astype(BF16), wa2_ref[...], preferred_element_type=F32)
        lg = jnp.dot(jax.nn.sigmoid(s_gate).astype(BF16), wg2_ref[...],
                     preferred_element_type=F32)
        for j in range(NT):
            LW[bi * NT + j] = lw[:, tile(j)]
            LA[bi * NT + j] = la[:, tile(j)]
            LG[bi * NT + j] = lg[:, tile(j)]
        car_g[bi] = zg_ref[bi, Tc - 1:Tc, :]
        car_s[bi] = zs_ref[bi, Tc - 1:Tc, :]

    for j in range(NT):
        for blk, dst in enumerate((R, K, None)):
            packed = _block_transpose([zr_ref[b, :, tile(blk * NT + j)] for b in range(NB)], RH)
            for g in range(NB):
                if dst is None:
                    VP[value_rows(NB * j + g)] = packed[g]
                else:
                    dst[NB * j + g] = packed[g]
        for src, dst in ((LW, W), (LA, BT)):
            packed = _block_transpose([src[b * NT + j] for b in range(NB)], RH)
            for g in range(NB):
                dst[NB * j + g] = packed[g]

    ss = jnp.zeros((Tc, LANES), F32)
    for n in range(N):
        raw = (R[n], K[n], VP[value_rows(n)])
        r, k, v = (shifted(raw[blk], car_r[blk * N + n:blk * N + n + 1, :],
                           mu_r_ref[blk * N + n:blk * N + n + 1, :]) for blk in range(3))
        for blk in range(3):
            car_r[blk * N + n:blk * N + n + 1, :] = raw[blk][Tc - 1:Tc, :]
        a = jax.nn.sigmoid(a0_ref[n:n + 1, :] + BT[n])
        kk = k * kkw_ref[n:n + 1, :]
        ss = ss + kk * kk
        R[n] = r
        VP[value_rows(n)] = v
        KK[n] = kk
        K[n] = k * (1.0 + (a - 1.0) * ka_ref[n:n + 1, :])
        BT[n] = a
        W[n] = jnp.exp(-DECAY_SCALE * jax.nn.sigmoid(w0_ref[n:n + 1, :] + W[n]))
    inv = 1.0 / jnp.maximum(jnp.sqrt(ss), 1e-12)
    for n in range(N):
        kk = KK[n] * inv
        KK[n] = kk
        BT[n] = kk * BT[n]

    def step(t, carry):
        row = pl.ds(t, 1)
        blk = pl.ds(pl.multiple_of(t * SUBLANES, SUBLANES), SUBLANES)
        acc = [[None, None] for _ in range(VH)]
        for key in range(N):
            kk_row = KK[key, row, :]
            for vh in range(VH):
                term = S[key, vh] * kk_row
                prev = acc[vh][key % 2]
                acc[vh][key % 2] = term if prev is None else prev + term
        sa = [-(acc[vh][0] + acc[vh][1]) for vh in range(VH)]
        vv = [VP[vh, blk, :] for vh in range(VH)]
        out = [[None, None] for _ in range(VH)]
        for key in range(N):
            w_row = W[key, row, :]
            b_row = BT[key, row, :]
            k_row = K[key, row, :]
            r_row = R[key, row, :]
            for vh in range(VH):
                s_new = S[key, vh] * w_row + sa[vh] * b_row + vv[vh] * k_row
                S[key, vh] = s_new
                term = s_new * r_row
                prev = out[vh][key % 2]
                out[vh][key % 2] = term if prev is None else prev + term
        for vh in range(VH):
            OP[vh, blk, :] = out[vh][0] + out[vh][1]
        return carry

    lax.fori_loop(0, Tc, step, 0)

    inv_n = 1.0 / N
    tot = jnp.zeros((Tc, LANES), F32)
    bon = jnp.zeros((Tc, LANES), F32)
    for n in range(N):
        tot = tot + OP[value_rows(n)]
        bon = bon + R[n] * K[n] * rk_ref[n:n + 1, :]
    mean = tot * inv_n
    var = jnp.zeros((Tc, LANES), F32)
    for n in range(N):
        d = OP[value_rows(n)] - mean
        var = var + d * d
    rstd = lax.rsqrt(var * inv_n + GN_EPS)
    for j in range(NT):
        ys = []
        for g in range(NB):
            n = NB * j + g
            o = (OP[value_rows(n)] - mean) * rstd * gnw_ref[n:n + 1, :] + gnb_ref[n:n + 1, :]
            ys.append(o + bon * VP[value_rows(n)])
        for b, y in enumerate(_block_transpose(ys, RH)):
            y_ref[b, :, tile(j)] = y * LG[b * NT + j]

    @pl.when(c == pl.num_programs(1) - 1)
    def _():
        s1_ref[...] = S[...]
        for b in range(NB):
            shn_r_ref[b] = zr_ref[b, Tc - 1:Tc, :]
        shn_g_ref[...] = car_g[...]
        shn_s_ref[...] = car_s[...]


def rwkv_branch(z, shp_r, sh_g, sh_s, S0, P, steps):
    B, T, _ = z.shape
    N = R_HEAD
    NB = RW_BATCH
    D = LANES // NB * N
    NT = D // LANES
    VH = N // SUBLANES
    assert B % NB == 0 and T % steps == 0 and steps % SUBLANES == 0
    g_blk = 8 * D // GATE_BLOCK

    def seq(width, j):
        return pl.BlockSpec((NB, steps, width), lambda g, c: (g, c, j))

    def per_batch(*shape):
        return pl.BlockSpec((NB,) + shape, lambda g, c: (g,) + (0,) * len(shape))

    def per_group(*shape):
        return pl.BlockSpec((None,) + shape, lambda g, c: (g,) + (0,) * len(shape))

    def const(*shape):
        return pl.BlockSpec(shape, lambda g, c: (0,) * len(shape))

    packed = pltpu.VMEM((N, steps, LANES), F32)
    natural = pltpu.VMEM((NB * NT, steps, LANES), F32)
    by_value = pltpu.VMEM((VH, steps * SUBLANES, LANES), F32)
    y, shn_r, shn_g, shn_s, S1 = pl.pallas_call(
        _rwkv_kernel,
        grid=(B // NB, T // steps),
        in_specs=[seq(3 * D, 1), seq(GATE_BLOCK, g_blk), seq(GATE_BLOCK, g_blk + 1),
                  per_group(3 * N, LANES), per_batch(1, GATE_BLOCK), per_batch(1, GATE_BLOCK),
                  const(3 * N, LANES), const(1, GATE_BLOCK), const(1, GATE_BLOCK),
                  const(GATE_BLOCK, D), const(GATE_BLOCK, D), const(GATE_BLOCK, D)]
                 + [const(N, LANES)] * 7
                 + [per_group(N, VH, SUBLANES, LANES)],
        out_specs=[seq(D, 0), per_batch(1, 3 * D), per_batch(1, GATE_BLOCK),
                   per_batch(1, GATE_BLOCK), per_group(N, VH, SUBLANES, LANES)],
        out_shape=[jax.ShapeDtypeStruct((B, T, D), F32),
                   jax.ShapeDtypeStruct((B, 1, 3 * D), F32),
                   jax.ShapeDtypeStruct((B, 1, GATE_BLOCK), F32),
                   jax.ShapeDtypeStruct((B, 1, GATE_BLOCK), F32),
                   jax.ShapeDtypeStruct((B // NB, N, VH, SUBLANES, LANES), F32)],
        scratch_shapes=[pltpu.VMEM((N, VH, SUBLANES, LANES), F32),
                        pltpu.VMEM((3 * N, LANES), F32), pltpu.VMEM((NB, 1, GATE_BLOCK), F32),
                        pltpu.VMEM((NB, 1, GATE_BLOCK), F32)]
                       + [natural] * 3 + [packed] * 5 + [by_value] * 2,
        compiler_params=_params("parallel", "arbitrary"),
        name="rwkv7",
    )(z, z, z, shp_r, sh_g, sh_s, P["mu_r"], P["mu_g"], P["mu_s"],
      P["w_w2"], P["w_a2"], P["w_g2"], P["w0"], P["a0"], P["k_k"], P["k_a"], P["r_k"],
      P["gn_w"], P["gn_b"], S0)
    return y, shn_r, shn_g, shn_s, S1


def _nh(x):
    lead = x.shape[:-1]
    return x.reshape(lead + (-1, R_HEAD)).swapaxes(-1, -2).reshape(x.shape)


def _hn(x):
    lead = x.shape[:-1]
    return x.reshape(lead + (R_HEAD, -1)).swapaxes(-1, -2).reshape(x.shape)


def _lane_vec(x):
    return jnp.tile(x.reshape(-1, R_HEAD).T, (1, RW_BATCH))


def _pack_rows(x):
    B = x.shape[0]
    x = x.reshape(B // RW_BATCH, RW_BATCH, -1, R_HEAD)
    return x.transpose(0, 3, 1, 2).reshape(B // RW_BATCH, R_HEAD, LANES)


def _pack_state(S):
    B, RH = S.shape[:2]
    S = S.reshape(B // RW_BATCH, RW_BATCH, RH, R_HEAD // SUBLANES, SUBLANES, R_HEAD)
    return S.transpose(0, 5, 3, 4, 1, 2).reshape(B // RW_BATCH, R_HEAD, R_HEAD // SUBLANES,
                                                 SUBLANES, LANES)


def _unpack_state(S):
    G = S.shape[0]
    RH = LANES // RW_BATCH
    S = S.reshape(G, R_HEAD, R_HEAD // SUBLANES, SUBLANES, RW_BATCH, RH)
    return S.transpose(0, 4, 5, 2, 3, 1).reshape(G * RW_BATCH, RH, R_HEAD, R_HEAD)


def _pick_tile(n, candidates):
    for c in candidates:
        if n % c == 0:
            return c
    return n


def _repack_w_in(w, D, lora_w, lora_a, lora_g):
    H = M_HEADS
    o_end = 3 * D
    f_end = o_end + 2 * H
    zr = f_end
    xw = zr + 3 * D
    xa = xw + lora_w
    xg = xa + lora_a
    ga = xg + lora_g
    small = lora_w + lora_a + 2 * H
    assert lora_g == GATE_BLOCK and small <= GATE_BLOCK
    pad = jnp.zeros((w.shape[0], GATE_BLOCK - small), w.dtype)
    rkv = [_nh(w[:, zr + i * D:zr + (i + 1) * D]) for i in range(3)]
    return jnp.concatenate(
        [w[:, :o_end]] + rkv + [w[:, ga:ga + D], _nh(w[:, ga + D:ga + 2 * D]), w[:, xg:ga],
                                w[:, xw:xg], w[:, o_end:f_end], pad], axis=1)


def _layer(h, p_l, conv0, C0, n0, m0, shift0, S0, chunk, W, final, g_final):
    B, T, D = h.shape
    M = B * T
    tm = min(512, M)
    lora_w, lora_a, _ = W["lora"]
    gate_off = lora_w + lora_a

    h2 = h.reshape(M, D)
    z2 = norm_matmul(h2, W["g_mix"], W["w_in"], min(1024, M), W["w_in_tn"])
    z = z2.reshape(B, T, -1)

    y_a, conv_new, C1, n1, m1 = mlstm_branch(z, conv0, C0, n0, m0, W["conv_w"], W["conv_b"],
                                             W["gate_bias"], W["m_norm_g"], chunk, gate_off)

    shp_r = jnp.concatenate([_pack_rows(shift0[:, 0, i * D:(i + 1) * D]) for i in range(3)], axis=1)
    sh_g = shift0[..., 3 * D + gate_off:]
    sh_s = jnp.pad(shift0[..., 3 * D:3 * D + gate_off], ((0, 0), (0, 0), (0, GATE_BLOCK - gate_off)))
    y_b, shn_r, shn_g, shn_s, S1 = rwkv_branch(z, shp_r, sh_g, sh_s, _pack_state(S0), W["rwkv"],
                                               min(32, T))
    shift_new = jnp.concatenate([_hn(shn_r[..., i * D:(i + 1) * D]) for i in range(3)]
                                + [shn_s[..., :gate_off], shn_g], axis=-1)
    S1 = _unpack_state(S1)

    h2 = merge_project(h2, z2, y_a.reshape(M, D), y_b.reshape(M, D), W["w_out"], W["w_out_nh"],
                       min(256, M))
    h2 = ffn(h2, W["g_ffn"], W["w_gate"], W["w_up"], W["w_down"], tm, W["ffn_tf"])
    h2 = ple(h2, W["g_ple"], p_l.reshape(M, -1), W["w_ple_gate"], W["w_ple_in"], g_final, tm, final)
    return h2.reshape(B, T, D), (conv_new, C1, n1, m1, shift_new, S1)


def kernel(x_prompt, x_sample, p_prompt, p_sample, state_mlstm_C, state_mlstm_n, state_mlstm_m, state_mlstm_conv, state_rwkv_wkv, state_rwkv_shift, g_mix, w_in, conv_w, conv_b, b_i, b_f, m_norm_g, rwkv_mu, w0, w_w2, a0, w_a2, w_g2, k_k, k_a, r_k, gn_w, gn_b, w_out, g_ffn, w_gate, w_up, w_down, g_ple, w_ple_gate, w_ple_in, g_final):
    depth = w_in.shape[0]
    D = x_prompt.shape[-1]
    H = M_HEADS
    DV = D // H
    DK = DV // 2
    QK = H * DK
    RH = D // R_HEAD
    r_shift = state_rwkv_shift.shape[-1]
    lora = (w_w2.shape[1], w_a2.shape[1], w_g2.shape[1])
    n_pack = 8 * D + 2 * GATE_BLOCK
    w_in_tn = _pick_tile(n_pack, (768, 512, 256, 128))

    layers = []
    for l in range(depth):
        gate_bias = jnp.zeros((1, GATE_BLOCK), F32)
        gate_bias = gate_bias.at[0, lora[0] + lora[1]:lora[0] + lora[1] + 2 * H].set(
            jnp.concatenate([b_i[l], b_f[l]]))
        mu = rwkv_mu[l]
        lw, la = lora[0], lora[1]
        w_w2p = jnp.zeros((GATE_BLOCK, D), F32).at[:lw].set(_nh(w_w2[l]))
        w_a2p = jnp.zeros((GATE_BLOCK, D), F32).at[lw:lw + la].set(_nh(w_a2[l]))
        rwkv = dict(
            mu_r=jnp.concatenate([_lane_vec(mu[i * D:(i + 1) * D]) for i in range(3)]),
            mu_g=mu[3 * D + lw + la:][None],
            mu_s=jnp.pad(mu[3 * D:3 * D + lw + la], (0, GATE_BLOCK - lw - la))[None],
            w_w2=w_w2p.astype(BF16), w_a2=w_a2p.astype(BF16), w_g2=_nh(w_g2[l]).astype(BF16),
            w0=_lane_vec(w0[l]), a0=_lane_vec(a0[l]), k_k=_lane_vec(k_k[l]),
            k_a=_lane_vec(k_a[l]), r_k=_lane_vec(r_k[l].reshape(-1)),
            gn_w=_lane_vec(gn_w[l]), gn_b=_lane_vec(gn_b[l]))
        layers.append(dict(
            g_mix=g_mix[l][None], w_in=_repack_w_in(w_in[l], D, *lora).astype(BF16),
            w_in_tn=w_in_tn, lora=lora,
            conv_w=conv_w[l], conv_b=conv_b[l][None], gate_bias=gate_bias,
            m_norm_g=m_norm_g[l][None], rwkv=rwkv,
            w_out=w_out[l].astype(BF16),
            w_out_nh=_nh(w_out[l].T).T.astype(BF16), g_ffn=g_ffn[l][None],
            w_gate=w_gate[l].astype(BF16), w_up=w_up[l].astype(BF16),
            w_down=w_down[l].astype(BF16), ffn_tf=_pick_tile(w_gate.shape[2], (512, 256, 128)),
            g_ple=g_ple[l][None], w_ple_gate=w_ple_gate[l].astype(BF16),
            w_ple_in=w_ple_in[l].astype(BF16)))
    gf = g_final[None]

    def run(x, p, conv0, C0, n0, m0, shift0, S0, chunk):
        h = x
        outs = ([], [], [], [], [], [])
        for l in range(depth):
            h, st = _layer(h, p[l], conv0[l], C0[l], n0[l], m0[l], shift0[l], S0[l], chunk,
                           layers[l], l == depth - 1, gf)
            for lst, s in zip(outs, st):
                lst.append(s)
        return h, [jnp.stack(o, axis=0) for o in outs]

    Bp = x_prompt.shape[0]
    y_prompt, (conv_p, C_p, n_p, m_p, shift_p, wkv_p) = run(
        x_prompt, p_prompt,
        jnp.zeros((depth, Bp, CONV_W - 1, 2 * QK), F32),
        jnp.zeros((depth, Bp, H, DK, DV), F32),
        jnp.zeros((depth, Bp, H, DK), F32),
        jnp.zeros((depth, Bp, H), F32),
        jnp.zeros((depth, Bp, 1, r_shift), F32),
        jnp.zeros((depth, Bp, RH, R_HEAD, R_HEAD), F32),
        CHUNK)
    y_sample, (conv_s, C_s, n_s, m_s, shift_s, wkv_s) = run(
        x_sample, p_sample, state_mlstm_conv, state_mlstm_C, state_mlstm_n, state_mlstm_m,
        state_rwkv_shift, state_rwkv_wkv, x_sample.shape[1])
    return (y_prompt, y_sample, C_p, n_p, m_p, conv_p, wkv_p, shift_p,
            C_s, n_s, m_s, conv_s, wkv_s, shift_s)
```

```python
import functools

import jax
import jax.numpy as jnp
from jax import lax
from jax.experimental import pallas as pl
from jax.experimental.pallas import tpu as pltpu

F32 = jnp.float32
BF16 = jnp.bfloat16

EPS = 1e-6
GN_EPS = 64e-5
M_HEADS = 4
CONV_W = 4
R_HEAD = 64
CHUNK = 64
LANES = 128
SUBLANES = 8
VMEM_LIMIT = 48 * 1024 * 1024

GATE_BLOCK = 256


def _params(*sem):
    return pltpu.CompilerParams(dimension_semantics=sem, vmem_limit_bytes=VMEM_LIMIT)


def _rms(x, g):
    return x * lax.rsqrt(jnp.mean(x * x, axis=-1, keepdims=True) + EPS) * g


def _log_sigmoid(x):
    return jnp.minimum(x, 0.0) - jnp.log1p(jnp.exp(-jnp.abs(x)))


def _norm_mm_kernel(x_ref, g_ref, w_ref, o_ref, u_ref):
    @pl.when(pl.program_id(1) == 0)
    def _():
        u_ref[...] = _rms(x_ref[...], g_ref[...]).astype(BF16)

    o_ref[...] = jnp.dot(u_ref[...], w_ref[...], preferred_element_type=F32)


def norm_matmul(x, g, w, tm, tn):
    M, K = x.shape
    N = w.shape[1]
    return pl.pallas_call(
        _norm_mm_kernel,
        grid=(M // tm, N // tn),
        in_specs=[pl.BlockSpec((tm, K), lambda i, j: (i, 0)),
                  pl.BlockSpec((1, K), lambda i, j: (0, 0)),
                  pl.BlockSpec((K, tn), lambda i, j: (0, j))],
        out_specs=pl.BlockSpec((tm, tn), lambda i, j: (i, j)),
        out_shape=jax.ShapeDtypeStruct((M, N), F32),
        scratch_shapes=[pltpu.VMEM((tm, K), BF16)],
        compiler_params=_params("parallel", "arbitrary"),
        name="norm_matmul",
    )(x, g, w)


def _merge_kernel(h_ref, ga_ref, gb_ref, ya_ref, yb_ref, wa_ref, wb_ref, o_ref):
    ma = (jax.nn.sigmoid(ga_ref[...]) * ya_ref[...]).astype(BF16)
    mb = (jax.nn.sigmoid(gb_ref[...]) * yb_ref[...]).astype(BF16)
    o_ref[...] = (h_ref[...] + jnp.dot(ma, wa_ref[...], preferred_element_type=F32)
                  + jnp.dot(mb, wb_ref[...], preferred_element_type=F32))


def merge_project(h, z, ya, yb, wa, wb, tm):
    M, D = h.shape
    row = pl.BlockSpec((tm, D), lambda i: (i, 0))
    wspec = pl.BlockSpec((D, D), lambda i: (0, 0), pipeline_mode=pl.Buffered(1))
    return pl.pallas_call(
        _merge_kernel,
        grid=(M // tm,),
        in_specs=[row,
                  pl.BlockSpec((tm, D), lambda i: (i, 6)),
                  pl.BlockSpec((tm, D), lambda i: (i, 7)),
                  row, row, wspec, wspec],
        out_specs=row,
        out_shape=jax.ShapeDtypeStruct((M, D), F32),
        compiler_params=_params("parallel"),
        name="merge_project",
    )(h, z, z, ya, yb, wa, wb)


def _ffn_kernel(h_ref, g_ref, wg_ref, wu_ref, wd_ref, o_ref, u_ref, acc_ref):
    f = pl.program_id(1)

    @pl.when(f == 0)
    def _():
        u_ref[...] = _rms(h_ref[...], g_ref[...]).astype(BF16)
        acc_ref[...] = jnp.zeros_like(acc_ref)

    u = u_ref[...]
    gate = jnp.dot(u, wg_ref[...], preferred_element_type=F32)
    up = jnp.dot(u, wu_ref[...], preferred_element_type=F32)
    act = (gate * jax.nn.sigmoid(gate) * up).astype(BF16)
    acc_ref[...] += jnp.dot(act, wd_ref[...], preferred_element_type=F32)

    @pl.when(f == pl.num_programs(1) - 1)
    def _():
        o_ref[...] = h_ref[...] + acc_ref[...]


def ffn(h, g, wg, wu, wd, tm, tf):
    M, D = h.shape
    F = wg.shape[1]
    return pl.pallas_call(
        _ffn_kernel,
        grid=(M // tm, F // tf),
        in_specs=[pl.BlockSpec((tm, D), lambda i, j: (i, 0)),
                  pl.BlockSpec((1, D), lambda i, j: (0, 0)),
                  pl.BlockSpec((D, tf), lambda i, j: (0, j)),
                  pl.BlockSpec((D, tf), lambda i, j: (0, j)),
                  pl.BlockSpec((tf, D), lambda i, j: (j, 0))],
        out_specs=pl.BlockSpec((tm, D), lambda i, j: (i, 0)),
        out_shape=jax.ShapeDtypeStruct((M, D), F32),
        scratch_shapes=[pltpu.VMEM((tm, D), BF16), pltpu.VMEM((tm, D), F32)],
        compiler_params=_params("parallel", "arbitrary"),
        name="ffn",
    )(h, g, wg, wu, wd)


def _ple_kernel(h_ref, g_ref, p_ref, wpg_ref, wpi_ref, gf_ref, o_ref, *, final):
    h = h_ref[...]
    u = _rms(h, g_ref[...]).astype(BF16)
    gate = jnp.dot(u, wpg_ref[...], preferred_element_type=F32)
    emb = jnp.dot(p_ref[...].astype(BF16), wpi_ref[...], preferred_element_type=F32)
    out = h + jax.nn.sigmoid(gate) * emb
    if final:
        out = _rms(out, gf_ref[...])
    o_ref[...] = out


def ple(h, g, p, wpg, wpi, g_final, tm, final):
    M, D = h.shape
    P = p.shape[1]
    return pl.pallas_call(
        functools.partial(_ple_kernel, final=final),
        grid=(M // tm,),
        in_specs=[pl.BlockSpec((tm, D), lambda i: (i, 0)),
                  pl.BlockSpec((1, D), lambda i: (0, 0)),
                  pl.BlockSpec((tm, P), lambda i: (i, 0)),
                  pl.BlockSpec((D, D), lambda i: (0, 0), pipeline_mode=pl.Buffered(1)),
                  pl.BlockSpec((P, D), lambda i: (0, 0), pipeline_mode=pl.Buffered(1)),
                  pl.BlockSpec((1, D), lambda i: (0, 0))],
        out_specs=pl.BlockSpec((tm, D), lambda i: (i, 0)),
        out_shape=jax.ShapeDtypeStruct((M, D), F32),
        compiler_params=_params("parallel"),
        name="ple",
    )(h, g, p, wpg, wpi, g_final)


def _mlstm_kernel(qk_ref, v_ref, o_ref, gt_ref, conv0_ref, cw_ref, cb_ref, gbias_ref, mg_ref,
                  c0_ref, n0_ref, m0_ref,
                  y_ref, convn_ref, c1_ref, n1_ref, m1_ref,
                  xp, C, n, m, *, gate_off):
    c = pl.program_id(1)
    L, D = qk_ref.shape
    H = M_HEADS
    DV = D // H
    DK = DV // 2
    QK = H * DK
    halo = CONV_W - 1
    top = SUBLANES - halo

    @pl.when(c == 0)
    def _():
        xp[top:SUBLANES, :] = conv0_ref[...]
        C[...] = c0_ref[...]
        n[...] = n0_ref[...]
        m[...] = m0_ref[...]

    xp[SUBLANES:SUBLANES + L, :] = qk_ref[...]
    gt = gt_ref[...] + gbias_ref[...]

    row = lax.broadcasted_iota(jnp.int32, (L, L), 0)
    col = lax.broadcasted_iota(jnp.int32, (L, L), 1)
    tril = row >= col
    eye = row == col

    def conv_silu(c0):
        acc = cb_ref[:, c0:c0 + DK] + xp[top:top + L, c0:c0 + DK] * cw_ref[0:1, c0:c0 + DK]
        for j in range(1, CONV_W):
            acc = acc + xp[top + j:top + j + L, c0:c0 + DK] * cw_ref[j:j + 1, c0:c0 + DK]
        return acc * jax.nn.sigmoid(acc)

    for h in range(H):
        q = conv_silu(h * DK)
        k = conv_silu(QK + h * DK) * (DK ** -0.5)
        vb = v_ref[:, h * DV:(h + 1) * DV].astype(BF16)
        li_c = gt[:, gate_off + h:gate_off + h + 1]
        lf_c = _log_sigmoid(gt[:, gate_off + H + h:gate_off + H + h + 1])
        li_r = jnp.sum(jnp.where(eye, li_c, 0.0), axis=0, keepdims=True)
        b_row = jnp.sum(jnp.where(row <= col, lf_c, 0.0), axis=0, keepdims=True)
        b_col = jnp.sum(jnp.where(eye, b_row, 0.0), axis=1, keepdims=True)
        m_old = m[h]
        a_col = b_col + m_old
        dmat = jnp.where(tril, b_col - b_row + li_r, -jnp.inf)
        mt = jnp.maximum(a_col, jnp.max(dmat, axis=1, keepdims=True))
        ea = jnp.exp(a_col - mt)
        p = jnp.exp(dmat - mt)

        qb = q.astype(BF16)
        Ch = C[h]
        nh = n[h]
        s = lax.dot_general(qb, k.astype(BF16), (((1,), (1,)), ((), ())),
                            preferred_element_type=F32) * p
        num = (ea * jnp.dot(qb, Ch.astype(BF16), preferred_element_type=F32)
               + jnp.dot(s.astype(BF16), vb, preferred_element_type=F32))
        den = ea * jnp.sum(q * nh, axis=1, keepdims=True) + jnp.sum(s, axis=1, keepdims=True)
        hh = num / jnp.maximum(jnp.abs(den), jnp.exp(-mt))
        hh = hh * lax.rsqrt(jnp.mean(hh * hh, axis=1, keepdims=True) + EPS)
        y_ref[:, h * DV:(h + 1) * DV] = (jax.nn.sigmoid(o_ref[:, h * DV:(h + 1) * DV])
                                         * (hh * mg_ref[:, h * DV:(h + 1) * DV]))

        m_new = mt[L - 1:L, :]
        dec = jnp.exp(a_col[L - 1:L, :] - m_new)
        wk = jnp.exp(b_col[L - 1:L, :] - b_col + li_c - m_new)
        kw = k * wk
        C[h] = dec * Ch + lax.dot_general(kw.astype(BF16), vb, (((0,), (0,)), ((), ())),
                                          preferred_element_type=F32)
        n[h] = dec * nh + jnp.sum(kw, axis=0, keepdims=True)
        m[h] = m_new

    xp[top:SUBLANES, :] = xp[top + L:SUBLANES + L, :]

    @pl.when(c == pl.num_programs(1) - 1)
    def _():
        convn_ref[...] = xp[top:SUBLANES, :]
        c1_ref[...] = C[...]
        n1_ref[...] = n[...]
        m1_ref[...] = m[...]


def mlstm_branch(z, conv0, C0, n0, m0, conv_w, conv_b, gate_bias, m_norm_g, L, gate_off):
    B, T, _ = z.shape
    H = M_HEADS
    _, _, DK, DV = C0.shape
    D = H * DV
    nc = T // L
    halo = CONV_W - 1

    def seq(j):
        return pl.BlockSpec((None, L, D), lambda b, c: (b, c, j))

    def per_batch(*shape):
        return pl.BlockSpec((None,) + shape, lambda b, c: (b,) + (0,) * len(shape))

    def const(*shape):
        return pl.BlockSpec(shape, lambda b, c: (0,) * len(shape))

    gate_blk = (8 * D + GATE_BLOCK) // GATE_BLOCK
    y, conv_new, C1, n1, m1 = pl.pallas_call(
        functools.partial(_mlstm_kernel, gate_off=gate_off),
        grid=(B, nc),
        in_specs=[seq(0), seq(1), seq(2),
                  pl.BlockSpec((None, L, GATE_BLOCK), lambda b, c: (b, c, gate_blk)),
                  per_batch(halo, D), const(CONV_W, D), const(1, D), const(1, GATE_BLOCK),
                  const(1, D),
                  per_batch(H, DK, DV), per_batch(H, 1, DK), per_batch(H, 1, 1)],
        out_specs=[seq(0), per_batch(halo, D),
                   per_batch(H, DK, DV), per_batch(H, 1, DK), per_batch(H, 1, 1)],
        out_shape=[jax.ShapeDtypeStruct((B, T, D), F32),
                   jax.ShapeDtypeStruct((B, halo, D), F32),
                   jax.ShapeDtypeStruct((B, H, DK, DV), F32),
                   jax.ShapeDtypeStruct((B, H, 1, DK), F32),
                   jax.ShapeDtypeStruct((B, H, 1, 1), F32)],
        scratch_shapes=[pltpu.VMEM((L + SUBLANES, D), F32), pltpu.VMEM((H, DK, DV), F32),
                        pltpu.VMEM((H, 1, DK), F32), pltpu.VMEM((H, 1, 1), F32)],
        compiler_params=_params("parallel", "arbitrary"),
        name="mlstm",
    )(z, z, z, z, conv0, conv_w, conv_b, gate_bias, m_norm_g,
      C0, n0[:, :, None, :], m0[:, :, None, None])
    return y, conv_new, C1, n1[:, :, 0, :], m1[:, :, 0, 0]


RW_BATCH = 4
DECAY_SCALE = 0.6065306597126334


def _block_transpose(a, rh):
    lane = lax.broadcasted_iota(jnp.int32, a[0].shape, 1)
    low = lane < 2 * rh
    even = (lane // rh) % 2 == 0
    c0 = jnp.where(low, a[0], pltpu.roll(a[2], 2 * rh, axis=1))
    c2 = jnp.where(low, pltpu.roll(a[0], 2 * rh, axis=1), a[2])
    c1 = jnp.where(low, a[1], pltpu.roll(a[3], 2 * rh, axis=1))
    c3 = jnp.where(low, pltpu.roll(a[1], 2 * rh, axis=1), a[3])
    return (jnp.where(even, c0, pltpu.roll(c1, rh, axis=1)),
            jnp.where(even, pltpu.roll(c0, 3 * rh, axis=1), c1),
            jnp.where(even, c2, pltpu.roll(c3, rh, axis=1)),
            jnp.where(even, pltpu.roll(c2, 3 * rh, axis=1), c3))


def _rwkv_kernel(zr_ref, zg_ref, zs_ref, shp_r_ref, sh_g_ref, sh_s_ref,
                 mu_r_ref, mu_g_ref, mu_s_ref, ww2_ref, wa2_ref, wg2_ref,
                 w0_ref, a0_ref, kkw_ref, ka_ref, rk_ref, gnw_ref, gnb_ref, s0_ref,
                 y_ref, shn_r_ref, shn_g_ref, shn_s_ref, s1_ref,
                 S, car_r, car_g, car_s, LW, LA, LG, R, W, K, KK, BT, VP, OP):
    c = pl.program_id(1)
    NB, Tc, D3 = zr_ref.shape
    D = D3 // 3
    RH = D // R_HEAD
    N = R_HEAD
    NT = D // LANES
    VH = N // SUBLANES

    @pl.when(c == 0)
    def _():
        S[...] = s0_ref[...]
        car_r[...] = shp_r_ref[...]
        car_g[...] = sh_g_ref[...]
        car_s[...] = sh_s_ref[...]

    first_row = lax.broadcasted_iota(jnp.int32, (Tc, LANES), 0) == 0

    def shifted(x, carry_row, mu):
        prev = jnp.where(first_row, carry_row, pltpu.roll(x, 1, axis=0))
        return x + mu * (prev - x)

    def tile(j):
        return slice(j * LANES, (j + 1) * LANES)

    def value_rows(n):
        return (n // SUBLANES, pl.ds(n % SUBLANES, Tc, stride=SUBLANES), slice(None))

    for bi in range(NB):
        xs = [shifted(zs_ref[bi, :, tile(j)], car_s[bi, :, tile(j)], mu_s_ref[:, tile(j)])
              for j in range(GATE_BLOCK // LANES)]
        xg = [shifted(zg_ref[bi, :, tile(j)], car_g[bi, :, tile(j)], mu_g_ref[:, tile(j)])
              for j in range(GATE_BLOCK // LANES)]
        s_small = jnp.concatenate(xs, axis=1)
        s_gate = jnp.concatenate(xg, axis=1)
        lw = jnp.dot(jnp.tanh(s_small).astype(BF16), ww2_ref[...], preferred_element_type=F32)
        la = jnp.dot(s_small.astype(BF16), wa2_ref[...], preferred_element_type=F32)
        lg = jnp.dot(jax.nn.sigmoid(s_gate).astype(BF16), wg2_ref[...],
                     preferred_element_type=F32)
        for j in range(NT):
            LW[bi * NT + j] = lw[:, tile(j)]
            LA[bi * NT + j] = la[:, tile(j)]
            LG[bi * NT + j] = lg[:, tile(j)]
        car_g[bi] = zg_ref[bi, Tc - 1:Tc, :]
        car_s[bi] = zs_ref[bi, Tc - 1:Tc, :]

    for j in range(NT):
        for blk, dst in enumerate((R, K, None)):
            packed = _block_transpose([zr_ref[b, :, tile(blk * NT + j)] for b in range(NB)], RH)
            for g in range(NB):
                if dst is None:
                    VP[value_rows(NB * j + g)] = packed[g]
                else:
                    dst[NB * j + g] = packed[g]
        for src, dst in ((LW, W), (LA, BT)):
            packed = _block_transpose([src[b * NT + j] for b in range(NB)], RH)
            for g in range(NB):
                dst[NB * j + g] = packed[g]

    ss = jnp.zeros((Tc, LANES), F32)
    for n in range(N):
        raw = (R[n], K[n], VP[value_rows(n)])
        r, k, v = (shifted(raw[blk], car_r[blk * N + n:blk * N + n + 1, :],
                           mu_r_ref[blk * N + n:blk * N + n + 1, :]) for blk in range(3))
        for blk in range(3):
            car_r[blk * N + n:blk * N + n + 1, :] = raw[blk][Tc - 1:Tc, :]
        a = jax.nn.sigmoid(a0_ref[n:n + 1, :] + BT[n])
        kk = k * kkw_ref[n:n + 1, :]
        ss = ss + kk * kk
        R[n] = r
        VP[value_rows(n)] = v
        KK[n] = kk
        K[n] = k * (1.0 + (a - 1.0) * ka_ref[n:n + 1, :])
        BT[n] = a
        W[n] = jnp.exp(-DECAY_SCALE * jax.nn.sigmoid(w0_ref[n:n + 1, :] + W[n]))
    inv = 1.0 / jnp.maximum(jnp.sqrt(ss), 1e-12)
    for n in range(N):
        kk = KK[n] * inv
        KK[n] = kk
        BT[n] = kk * BT[n]

    def step(t, carry):
        row = pl.ds(t, 1)
        blk = pl.ds(pl.multiple_of(t * SUBLANES, SUBLANES), SUBLANES)
        acc = [[None, None] for _ in range(VH)]
        for key in range(N):
            kk_row = KK[key, row, :]
            for vh in range(VH):
                term = S[key, vh] * kk_row
                prev = acc[vh][key % 2]
                acc[vh][key % 2] = term if prev is None else prev + term
        sa = [-(acc[vh][0] + acc[vh][1]) for vh in range(VH)]
        vv = [VP[vh, blk, :] for vh in range(VH)]
        out = [[None, None] for _ in range(VH)]
        for key in range(N):
            w_row = W[key, row, :]
            b_row = BT[key, row, :]
            k_row = K[key, row, :]
            r_row = R[key, row, :]
            for vh in range(VH):
                s_new = S[key, vh] * w_row + sa[vh] * b_row + vv[vh] * k_row
                S[key, vh] = s_new
                term = s_new * r_row
                prev = out[vh][key % 2]
                out[vh][key % 2] = term if prev is None else prev + term
        for vh in range(VH):
            OP[vh, blk, :] = out[vh][0] + out[vh][1]
        return carry

    lax.fori_loop(0, Tc, step, 0, unroll=4)

    inv_n = 1.0 / N
    tot = jnp.zeros((Tc, LANES), F32)
    bon = jnp.zeros((Tc, LANES), F32)
    for n in range(N):
        tot = tot + OP[value_rows(n)]
        bon = bon + R[n] * K[n] * rk_ref[n:n + 1, :]
    mean = tot * inv_n
    var = jnp.zeros((Tc, LANES), F32)
    for n in range(N):
        d = OP[value_rows(n)] - mean
        var = var + d * d
    rstd = lax.rsqrt(var * inv_n + GN_EPS)
    for j in range(NT):
        ys = []
        for g in range(NB):
            n = NB * j + g
            o = (OP[value_rows(n)] - mean) * rstd * gnw_ref[n:n + 1, :] + gnb_ref[n:n + 1, :]
            ys.append(o + bon * VP[value_rows(n)])
        for b, y in enumerate(_block_transpose(ys, RH)):
            y_ref[b, :, tile(j)] = y * LG[b * NT + j]

    @pl.when(c == pl.num_programs(1) - 1)
    def _():
        s1_ref[...] = S[...]
        for b in range(NB):
            shn_r_ref[b] = zr_ref[b, Tc - 1:Tc, :]
        shn_g_ref[...] = car_g[...]
        shn_s_ref[...] = car_s[...]


def rwkv_branch(z, shp_r, sh_g, sh_s, S0, P, steps):
    B, T, _ = z.shape
    N = R_HEAD
    NB = RW_BATCH
    D = LANES // NB * N
    NT = D // LANES
    VH = N // SUBLANES
    assert B % NB == 0 and T % steps == 0 and steps % SUBLANES == 0
    g_blk = 8 * D // GATE_BLOCK

    def seq(width, j):
        return pl.BlockSpec((NB, steps, width), lambda g, c: (g, c, j))

    def per_batch(*shape):
        return pl.BlockSpec((NB,) + shape, lambda g, c: (g,) + (0,) * len(shape))

    def per_group(*shape):
        return pl.BlockSpec((None,) + shape, lambda g, c: (g,) + (0,) * len(shape))

    def const(*shape):
        return pl.BlockSpec(shape, lambda g, c: (0,) * len(shape))

    packed = pltpu.VMEM((N, steps, LANES), F32)
    natural = pltpu.VMEM((NB * NT, steps, LANES), F32)
    by_value = pltpu.VMEM((VH, steps * SUBLANES, LANES), F32)
    y, shn_r, shn_g, shn_s, S1 = pl.pallas_call(
        _rwkv_kernel,
        grid=(B // NB, T // steps),
        in_specs=[seq(3 * D, 1), seq(GATE_BLOCK, g_blk), seq(GATE_BLOCK, g_blk + 1),
                  per_group(3 * N, LANES), per_batch(1, GATE_BLOCK), per_batch(1, GATE_BLOCK),
                  const(3 * N, LANES), const(1, GATE_BLOCK), const(1, GATE_BLOCK),
                  const(GATE_BLOCK, D), const(GATE_BLOCK, D), const(GATE_BLOCK, D)]
                 + [const(N, LANES)] * 7
                 + [per_group(N, VH, SUBLANES, LANES)],
        out_specs=[seq(D, 0), per_batch(1, 3 * D), per_batch(1, GATE_BLOCK),
                   per_batch(1, GATE_BLOCK), per_group(N, VH, SUBLANES, LANES)],
        out_shape=[jax.ShapeDtypeStruct((B, T, D), F32),
                   jax.ShapeDtypeStruct((B, 1, 3 * D), F32),
                   jax.ShapeDtypeStruct((B, 1, GATE_BLOCK), F32),
                   jax.ShapeDtypeStruct((B, 1, GATE_BLOCK), F32),
                   jax.ShapeDtypeStruct((B // NB, N, VH, SUBLANES, LANES), F32)],
        scratch_shapes=[pltpu.VMEM((N, VH, SUBLANES, LANES), F32),
                        pltpu.VMEM((3 * N, LANES), F32), pltpu.VMEM((NB, 1, GATE_BLOCK), F32),
                        pltpu.VMEM((NB, 1, GATE_BLOCK), F32)]
                       + [natural] * 3 + [packed] * 5 + [by_value] * 2,
        compiler_params=_params("parallel", "arbitrary"),
        name="rwkv7",
    )(z, z, z, shp_r, sh_g, sh_s, P["mu_r"], P["mu_g"], P["mu_s"],
      P["w_w2"], P["w_a2"], P["w_g2"], P["w0"], P["a0"], P["k_k"], P["k_a"], P["r_k"],
      P["gn_w"], P["gn_b"], S0)
    return y, shn_r, shn_g, shn_s, S1


def _nh(x):
    lead = x.shape[:-1]
    return x.reshape(lead + (-1, R_HEAD)).swapaxes(-1, -2).reshape(x.shape)


def _hn(x):
    lead = x.shape[:-1]
    return x.reshape(lead + (R_HEAD, -1)).swapaxes(-1, -2).reshape(x.shape)


def _lane_vec(x):
    return jnp.tile(x.reshape(-1, R_HEAD).T, (1, RW_BATCH))


def _pack_rows(x):
    B = x.shape[0]
    x = x.reshape(B // RW_BATCH, RW_BATCH, -1, R_HEAD)
    return x.transpose(0, 3, 1, 2).reshape(B // RW_BATCH, R_HEAD, LANES)


def _pack_state(S):
    B, RH = S.shape[:2]
    S = S.reshape(B // RW_BATCH, RW_BATCH, RH, R_HEAD // SUBLANES, SUBLANES, R_HEAD)
    return S.transpose(0, 5, 3, 4, 1, 2).reshape(B // RW_BATCH, R_HEAD, R_HEAD // SUBLANES,
                                                 SUBLANES, LANES)


def _unpack_state(S):
    G = S.shape[0]
    RH = LANES // RW_BATCH
    S = S.reshape(G, R_HEAD, R_HEAD // SUBLANES, SUBLANES, RW_BATCH, RH)
    return S.transpose(0, 4, 5, 2, 3, 1).reshape(G * RW_BATCH, RH, R_HEAD, R_HEAD)


def _pick_tile(n, candidates):
    for c in candidates:
        if n % c == 0:
            return c
    return n


def _repack_w_in(w, D, lora_w, lora_a, lora_g):
    H = M_HEADS
    o_end = 3 * D
    f_end = o_end + 2 * H
    zr = f_end
    xw = zr + 3 * D
    xa = xw + lora_w
    xg = xa + lora_a
    ga = xg + lora_g
    small = lora_w + lora_a + 2 * H
    assert lora_g == GATE_BLOCK and small <= GATE_BLOCK
    pad = jnp.zeros((w.shape[0], GATE_BLOCK - small), w.dtype)
    rkv = [_nh(w[:, zr + i * D:zr + (i + 1) * D]) for i in range(3)]
    return jnp.concatenate(
        [w[:, :o_end]] + rkv + [w[:, ga:ga + D], _nh(w[:, ga + D:ga + 2 * D]), w[:, xg:ga],
                                w[:, xw:xg], w[:, o_end:f_end], pad], axis=1)


def _layer(h, p_l, conv0, C0, n0, m0, shift0, S0, chunk, W, final, g_final):
    B, T, D = h.shape
    M = B * T
    tm = min(512, M)
    lora_w, lora_a, _ = W["lora"]
    gate_off = lora_w + lora_a

    h2 = h.reshape(M, D)
    z2 = norm_matmul(h2, W["g_mix"], W["w_in"], min(1024, M), W["w_in_tn"])
    z = z2.reshape(B, T, -1)

    y_a, conv_new, C1, n1, m1 = mlstm_branch(z, conv0, C0, n0, m0, W["conv_w"], W["conv_b"],
                                             W["gate_bias"], W["m_norm_g"], chunk, gate_off)

    shp_r = jnp.concatenate([_pack_rows(shift0[:, 0, i * D:(i + 1) * D]) for i in range(3)], axis=1)
    sh_g = shift0[..., 3 * D + gate_off:]
    sh_s = jnp.pad(shift0[..., 3 * D:3 * D + gate_off], ((0, 0), (0, 0), (0, GATE_BLOCK - gate_off)))
    y_b, shn_r, shn_g, shn_s, S1 = rwkv_branch(z, shp_r, sh_g, sh_s, _pack_state(S0), W["rwkv"],
                                               min(32, T))
    shift_new = jnp.concatenate([_hn(shn_r[..., i * D:(i + 1) * D]) for i in range(3)]
                                + [shn_s[..., :gate_off], shn_g], axis=-1)
    S1 = _unpack_state(S1)

    h2 = merge_project(h2, z2, y_a.reshape(M, D), y_b.reshape(M, D), W["w_out"], W["w_out_nh"],
                       min(256, M))
    h2 = ffn(h2, W["g_ffn"], W["w_gate"], W["w_up"], W["w_down"], tm, W["ffn_tf"])
    h2 = ple(h2, W["g_ple"], p_l.reshape(M, -1), W["w_ple_gate"], W["w_ple_in"], g_final, tm, final)
    return h2.reshape(B, T, D), (conv_new, C1, n1, m1, shift_new, S1)


def kernel(x_prompt, x_sample, p_prompt, p_sample, state_mlstm_C, state_mlstm_n, state_mlstm_m, state_mlstm_conv, state_rwkv_wkv, state_rwkv_shift, g_mix, w_in, conv_w, conv_b, b_i, b_f, m_norm_g, rwkv_mu, w0, w_w2, a0, w_a2, w_g2, k_k, k_a, r_k, gn_w, gn_b, w_out, g_ffn, w_gate, w_up, w_down, g_ple, w_ple_gate, w_ple_in, g_final):
    depth = w_in.shape[0]
    D = x_prompt.shape[-1]
    H = M_HEADS
    DV = D // H
    DK = DV // 2
    QK = H * DK
    RH = D // R_HEAD
    r_shift = state_rwkv_shift.shape[-1]
    lora = (w_w2.shape[1], w_a2.shape[1], w_g2.shape[1])
    n_pack = 8 * D + 2 * GATE_BLOCK
    w_in_tn = _pick_tile(n_pack, (768, 512, 256, 128))

    layers = []
    for l in range(depth):
        gate_bias = jnp.zeros((1, GATE_BLOCK), F32)
        gate_bias = gate_bias.at[0, lora[0] + lora[1]:lora[0] + lora[1] + 2 * H].set(
            jnp.concatenate([b_i[l], b_f[l]]))
        mu = rwkv_mu[l]
        lw, la = lora[0], lora[1]
        w_w2p = jnp.zeros((GATE_BLOCK, D), F32).at[:lw].set(_nh(w_w2[l]))
        w_a2p = jnp.zeros((GATE_BLOCK, D), F32).at[lw:lw + la].set(_nh(w_a2[l]))
        rwkv = dict(
            mu_r=jnp.concatenate([_lane_vec(mu[i * D:(i + 1) * D]) for i in range(3)]),
            mu_g=mu[3 * D + lw + la:][None],
            mu_s=jnp.pad(mu[3 * D:3 * D + lw + la], (0, GATE_BLOCK - lw - la))[None],
            w_w2=w_w2p.astype(BF16), w_a2=w_a2p.astype(BF16), w_g2=_nh(w_g2[l]).astype(BF16),
            w0=_lane_vec(w0[l]), a0=_lane_vec(a0[l]), k_k=_lane_vec(k_k[l]),
            k_a=_lane_vec(k_a[l]), r_k=_lane_vec(r_k[l].reshape(-1)),
            gn_w=_lane_vec(gn_w[l]), gn_b=_lane_vec(gn_b[l]))
        layers.append(dict(
            g_mix=g_mix[l][None], w_in=_repack_w_in(w_in[l], D, *lora).astype(BF16),
            w_in_tn=w_in_tn, lora=lora,
            conv_w=conv_w[l], conv_b=conv_b[l][None], gate_bias=gate_bias,
            m_norm_g=m_norm_g[l][None], rwkv=rwkv,
            w_out=w_out[l].astype(BF16),
            w_out_nh=_nh(w_out[l].T).T.astype(BF16), g_ffn=g_ffn[l][None],
            w_gate=w_gate[l].astype(BF16), w_up=w_up[l].astype(BF16),
            w_down=w_down[l].astype(BF16), ffn_tf=_pick_tile(w_gate.shape[2], (512, 256, 128)),
            g_ple=g_ple[l][None], w_ple_gate=w_ple_gate[l].astype(BF16),
            w_ple_in=w_ple_in[l].astype(BF16)))
    gf = g_final[None]

    def run(x, p, conv0, C0, n0, m0, shift0, S0, chunk):
        h = x
        outs = ([], [], [], [], [], [])
        for l in range(depth):
            h, st = _layer(h, p[l], conv0[l], C0[l], n0[l], m0[l], shift0[l], S0[l], chunk,
                           layers[l], l == depth - 1, gf)
            for lst, s in zip(outs, st):
                lst.append(s)
        return h, [jnp.stack(o, axis=0) for o in outs]

    Bp = x_prompt.shape[0]
    y_prompt, (conv_p, C_p, n_p, m_p, shift_p, wkv_p) = run(
        x_prompt, p_prompt,
        jnp.zeros((depth, Bp, CONV_W - 1, 2 * QK), F32),
        jnp.zeros((depth, Bp, H, DK, DV), F32),
        jnp.zeros((depth, Bp, H, DK), F32),
        jnp.zeros((depth, Bp, H), F32),
        jnp.zeros((depth, Bp, 1, r_shift), F32),
        jnp.zeros((depth, Bp, RH, R_HEAD, R_HEAD), F32),
        CHUNK)
    y_sample, (conv_s, C_s, n_s, m_s, shift_s, wkv_s) = run(
        x_sample, p_sample, state_mlstm_conv, state_mlstm_C, state_mlstm_n, state_mlstm_m,
        state_rwkv_shift, state_rwkv_wkv, x_sample.shape[1])
    return (y_prompt, y_sample, C_p, n_p, m_p, conv_p, wkv_p, shift_p,
            C_s, n_s, m_s, conv_s, wkv_s, shift_s)
```

```python
import functools

import jax
import jax.numpy as jnp
from jax import lax
from jax.experimental import pallas as pl
from jax.experimental.pallas import tpu as pltpu

F32 = jnp.float32
BF16 = jnp.bfloat16

EPS = 1e-6
GN_EPS = 64e-5
M_HEADS = 4
CONV_W = 4
R_HEAD = 64
CHUNK = 64
LANES = 128
SUBLANES = 8
VMEM_LIMIT = 48 * 1024 * 1024

GATE_BLOCK = 256


def _params(*sem):
    return pltpu.CompilerParams(dimension_semantics=sem, vmem_limit_bytes=VMEM_LIMIT)


def _rms(x, g):
    return x * lax.rsqrt(jnp.mean(x * x, axis=-1, keepdims=True) + EPS) * g


def _log_sigmoid(x):
    return jnp.minimum(x, 0.0) - jnp.log1p(jnp.exp(-jnp.abs(x)))


def _norm_mm_kernel(x_ref, g_ref, w_ref, o_ref, u_ref):
    @pl.when(pl.program_id(1) == 0)
    def _():
        u_ref[...] = _rms(x_ref[...], g_ref[...]).astype(BF16)

    o_ref[...] = jnp.dot(u_ref[...], w_ref[...], preferred_element_type=F32)


def norm_matmul(x, g, w, tm, tn):
    M, K = x.shape
    N = w.shape[1]
    return pl.pallas_call(
        _norm_mm_kernel,
        grid=(M // tm, N // tn),
        in_specs=[pl.BlockSpec((tm, K), lambda i, j: (i, 0)),
                  pl.BlockSpec((1, K), lambda i, j: (0, 0)),
                  pl.BlockSpec((K, tn), lambda i, j: (0, j))],
        out_specs=pl.BlockSpec((tm, tn), lambda i, j: (i, j)),
        out_shape=jax.ShapeDtypeStruct((M, N), F32),
        scratch_shapes=[pltpu.VMEM((tm, K), BF16)],
        compiler_params=_params("parallel", "arbitrary"),
        name="norm_matmul",
    )(x, g, w)


def _merge_kernel(h_ref, ga_ref, gb_ref, ya_ref, yb_ref, wa_ref, wb_ref, o_ref):
    ma = (jax.nn.sigmoid(ga_ref[...]) * ya_ref[...]).astype(BF16)
    mb = (jax.nn.sigmoid(gb_ref[...]) * yb_ref[...]).astype(BF16)
    o_ref[...] = (h_ref[...] + jnp.dot(ma, wa_ref[...], preferred_element_type=F32)
                  + jnp.dot(mb, wb_ref[...], preferred_element_type=F32))


def merge_project(h, z, ya, yb, wa, wb, tm):
    M, D = h.shape
    row = pl.BlockSpec((tm, D), lambda i: (i, 0))
    wspec = pl.BlockSpec((D, D), lambda i: (0, 0), pipeline_mode=pl.Buffered(1))
    return pl.pallas_call(
        _merge_kernel,
        grid=(M // tm,),
        in_specs=[row,
                  pl.BlockSpec((tm, D), lambda i: (i, 6)),
                  pl.BlockSpec((tm, D), lambda i: (i, 7)),
                  row, row, wspec, wspec],
        out_specs=row,
        out_shape=jax.ShapeDtypeStruct((M, D), F32),
        compiler_params=_params("parallel"),
        name="merge_project",
    )(h, z, z, ya, yb, wa, wb)


def _ffn_kernel(h_ref, g_ref, wg_ref, wu_ref, wd_ref, o_ref, u_ref, acc_ref):
    f = pl.program_id(1)

    @pl.when(f == 0)
    def _():
        u_ref[...] = _rms(h_ref[...], g_ref[...]).astype(BF16)
        acc_ref[...] = jnp.zeros_like(acc_ref)

    u = u_ref[...]
    gate = jnp.dot(u, wg_ref[...], preferred_element_type=F32)
    up = jnp.dot(u, wu_ref[...], preferred_element_type=F32)
    act = (gate * jax.nn.sigmoid(gate) * up).astype(BF16)
    acc_ref[...] += jnp.dot(act, wd_ref[...], preferred_element_type=F32)

    @pl.when(f == pl.num_programs(1) - 1)
    def _():
        o_ref[...] = h_ref[...] + acc_ref[...]


def ffn(h, g, wg, wu, wd, tm, tf):
    M, D = h.shape
    F = wg.shape[1]
    return pl.pallas_call(
        _ffn_kernel,
        grid=(M // tm, F // tf),
        in_specs=[pl.BlockSpec((tm, D), lambda i, j: (i, 0)),
                  pl.BlockSpec((1, D), lambda i, j: (0, 0)),
                  pl.BlockSpec((D, tf), lambda i, j: (0, j)),
                  pl.BlockSpec((D, tf), lambda i, j: (0, j)),
                  pl.BlockSpec((tf, D), lambda i, j: (j, 0))],
        out_specs=pl.BlockSpec((tm, D), lambda i, j: (i, 0)),
        out_shape=jax.ShapeDtypeStruct((M, D), F32),
        scratch_shapes=[pltpu.VMEM((tm, D), BF16), pltpu.VMEM((tm, D), F32)],
        compiler_params=_params("parallel", "arbitrary"),
        name="ffn",
    )(h, g, wg, wu, wd)


def _ple_kernel(h_ref, g_ref, p_ref, wpg_ref, wpi_ref, gf_ref, o_ref, *, final):
    h = h_ref[...]
    u = _rms(h, g_ref[...]).astype(BF16)
    gate = jnp.dot(u, wpg_ref[...], preferred_element_type=F32)
    emb = jnp.dot(p_ref[...].astype(BF16), wpi_ref[...], preferred_element_type=F32)
    out = h + jax.nn.sigmoid(gate) * emb
    if final:
        out = _rms(out, gf_ref[...])
    o_ref[...] = out


def ple(h, g, p, wpg, wpi, g_final, tm, final):
    M, D = h.shape
    P = p.shape[1]
    return pl.pallas_call(
        functools.partial(_ple_kernel, final=final),
        grid=(M // tm,),
        in_specs=[pl.BlockSpec((tm, D), lambda i: (i, 0)),
                  pl.BlockSpec((1, D), lambda i: (0, 0)),
                  pl.BlockSpec((tm, P), lambda i: (i, 0)),
                  pl.BlockSpec((D, D), lambda i: (0, 0), pipeline_mode=pl.Buffered(1)),
                  pl.BlockSpec((P, D), lambda i: (0, 0), pipeline_mode=pl.Buffered(1)),
                  pl.BlockSpec((1, D), lambda i: (0, 0))],
        out_specs=pl.BlockSpec((tm, D), lambda i: (i, 0)),
        out_shape=jax.ShapeDtypeStruct((M, D), F32),
        compiler_params=_params("parallel"),
        name="ple",
    )(h, g, p, wpg, wpi, g_final)


def _mlstm_kernel(qk_ref, v_ref, o_ref, gt_ref, conv0_ref, cw_ref, cb_ref, gbias_ref, mg_ref,
                  c0_ref, n0_ref, m0_ref,
                  y_ref, convn_ref, c1_ref, n1_ref, m1_ref,
                  xp, C, n, m, *, gate_off):
    c = pl.program_id(1)
    L, D = qk_ref.shape
    H = M_HEADS
    DV = D // H
    DK = DV // 2
    QK = H * DK
    halo = CONV_W - 1
    top = SUBLANES - halo

    @pl.when(c == 0)
    def _():
        xp[top:SUBLANES, :] = conv0_ref[...]
        C[...] = c0_ref[...]
        n[...] = n0_ref[...]
        m[...] = m0_ref[...]

    xp[SUBLANES:SUBLANES + L, :] = qk_ref[...]
    gt = gt_ref[...] + gbias_ref[...]

    row = lax.broadcasted_iota(jnp.int32, (L, L), 0)
    col = lax.broadcasted_iota(jnp.int32, (L, L), 1)
    tril = row >= col
    eye = row == col

    def conv_silu(c0):
        acc = cb_ref[:, c0:c0 + DK] + xp[top:top + L, c0:c0 + DK] * cw_ref[0:1, c0:c0 + DK]
        for j in range(1, CONV_W):
            acc = acc + xp[top + j:top + j + L, c0:c0 + DK] * cw_ref[j:j + 1, c0:c0 + DK]
        return acc * jax.nn.sigmoid(acc)

    for h in range(H):
        q = conv_silu(h * DK)
        k = conv_silu(QK + h * DK) * (DK ** -0.5)
        vb = v_ref[:, h * DV:(h + 1) * DV].astype(BF16)
        li_c = gt[:, gate_off + h:gate_off + h + 1]
        lf_c = _log_sigmoid(gt[:, gate_off + H + h:gate_off + H + h + 1])
        li_r = jnp.sum(jnp.where(eye, li_c, 0.0), axis=0, keepdims=True)
        b_row = jnp.sum(jnp.where(row <= col, lf_c, 0.0), axis=0, keepdims=True)
        b_col = jnp.sum(jnp.where(eye, b_row, 0.0), axis=1, keepdims=True)
        m_old = m[h]
        a_col = b_col + m_old
        dmat = jnp.where(tril, b_col - b_row + li_r, -jnp.inf)
        mt = jnp.maximum(a_col, jnp.max(dmat, axis=1, keepdims=True))
        ea = jnp.exp(a_col - mt)
        p = jnp.exp(dmat - mt)

        qb = q.astype(BF16)
        Ch = C[h]
        nh = n[h]
        s = lax.dot_general(qb, k.astype(BF16), (((1,), (1,)), ((), ())),
                            preferred_element_type=F32) * p
        num = (ea * jnp.dot(qb, Ch.astype(BF16), preferred_element_type=F32)
               + jnp.dot(s.astype(BF16), vb, preferred_element_type=F32))
        den = ea * jnp.sum(q * nh, axis=1, keepdims=True) + jnp.sum(s, axis=1, keepdims=True)
        hh = num / jnp.maximum(jnp.abs(den), jnp.exp(-mt))
        hh = hh * lax.rsqrt(jnp.mean(hh * hh, axis=1, keepdims=True) + EPS)
        y_ref[:, h * DV:(h + 1) * DV] = (jax.nn.sigmoid(o_ref[:, h * DV:(h + 1) * DV])
                                         * (hh * mg_ref[:, h * DV:(h + 1) * DV]))

        m_new = mt[L - 1:L, :]
        dec = jnp.exp(a_col[L - 1:L, :] - m_new)
        wk = jnp.exp(b_col[L - 1:L, :] - b_col + li_c - m_new)
        kw = k * wk
        C[h] = dec * Ch + lax.dot_general(kw.astype(BF16), vb, (((0,), (0,)), ((), ())),
                                          preferred_element_type=F32)
        n[h] = dec * nh + jnp.sum(kw, axis=0, keepdims=True)
        m[h] = m_new

    xp[top:SUBLANES, :] = xp[top + L:SUBLANES + L, :]

    @pl.when(c == pl.num_programs(1) - 1)
    def _():
        convn_ref[...] = xp[top:SUBLANES, :]
        c1_ref[...] = C[...]
        n1_ref[...] = n[...]
        m1_ref[...] = m[...]


def mlstm_branch(z, conv0, C0, n0, m0, conv_w, conv_b, gate_bias, m_norm_g, L, gate_off):
    B, T, _ = z.shape
    H = M_HEADS
    _, _, DK, DV = C0.shape
    D = H * DV
    nc = T // L
    halo = CONV_W - 1

    def seq(j):
        return pl.BlockSpec((None, L, D), lambda b, c: (b, c, j))

    def per_batch(*shape):
        return pl.BlockSpec((None,) + shape, lambda b, c: (b,) + (0,) * len(shape))

    def const(*shape):
        return pl.BlockSpec(shape, lambda b, c: (0,) * len(shape))

    gate_blk = (8 * D + GATE_BLOCK) // GATE_BLOCK
    y, conv_new, C1, n1, m1 = pl.pallas_call(
        functools.partial(_mlstm_kernel, gate_off=gate_off),
        grid=(B, nc),
        in_specs=[seq(0), seq(1), seq(2),
                  pl.BlockSpec((None, L, GATE_BLOCK), lambda b, c: (b, c, gate_blk)),
                  per_batch(halo, D), const(CONV_W, D), const(1, D), const(1, GATE_BLOCK),
                  const(1, D),
                  per_batch(H, DK, DV), per_batch(H, 1, DK), per_batch(H, 1, 1)],
        out_specs=[seq(0), per_batch(halo, D),
                   per_batch(H, DK, DV), per_batch(H, 1, DK), per_batch(H, 1, 1)],
        out_shape=[jax.ShapeDtypeStruct((B, T, D), F32),
                   jax.ShapeDtypeStruct((B, halo, D), F32),
                   jax.ShapeDtypeStruct((B, H, DK, DV), F32),
                   jax.ShapeDtypeStruct((B, H, 1, DK), F32),
                   jax.ShapeDtypeStruct((B, H, 1, 1), F32)],
        scratch_shapes=[pltpu.VMEM((L + SUBLANES, D), F32), pltpu.VMEM((H, DK, DV), F32),
                        pltpu.VMEM((H, 1, DK), F32), pltpu.VMEM((H, 1, 1), F32)],
        compiler_params=_params("parallel", "arbitrary"),
        name="mlstm",
    )(z, z, z, z, conv0, conv_w, conv_b, gate_bias, m_norm_g,
      C0, n0[:, :, None, :], m0[:, :, None, None])
    return y, conv_new, C1, n1[:, :, 0, :], m1[:, :, 0, 0]


RW_BATCH = 4
DECAY_SCALE = 0.6065306597126334


def _block_transpose(a, rh):
    blk = lax.broadcasted_iota(jnp.int32, a[0].shape, 1) // rh
    out = []
    for g in range(4):
        acc = None
        for b in range(4):
            src = a[b] if b == g else pltpu.roll(a[b], ((b - g) % 4) * rh, axis=1)
            acc = src if acc is None else jnp.where(blk == b, src, acc)
        out.append(acc)
    return tuple(out)


def _rwkv_kernel(zr_ref, zg_ref, zs_ref, shp_r_ref, sh_g_ref, sh_s_ref,
                 mu_r_ref, mu_g_ref, mu_s_ref, ww2_ref, wa2_ref, wg2_ref,
                 w0_ref, a0_ref, kkw_ref, ka_ref, rk_ref, gnw_ref, gnb_ref, s0_ref,
                 y_ref, shn_r_ref, shn_g_ref, shn_s_ref, s1_ref,
                 S, car_r, car_g, car_s, PL, LW, LA, LG, R, W, K, KK, BT, VP, OP):
    c = pl.program_id(1)
    NB, Tc, D3 = zr_ref.shape
    D = D3 // 3
    RH = D // R_HEAD
    N = R_HEAD
    NT = D // LANES
    VH = N // SUBLANES

    @pl.when(c == 0)
    def _():
        S[...] = s0_ref[...]
        car_r[...] = shp_r_ref[...]
        car_g[...] = sh_g_ref[...]
        car_s[...] = sh_s_ref[...]

    first_row = lax.broadcasted_iota(jnp.int32, (Tc, LANES), 0) == 0

    def shifted(x, carry_row, mu):
        prev = jnp.where(first_row, carry_row, pltpu.roll(x, 1, axis=0))
        return x + mu * (prev - x)

    def tile(j):
        return slice(j * LANES, (j + 1) * LANES)

    def value_rows(n):
        return (n // SUBLANES, pl.ds(n % SUBLANES, Tc, stride=SUBLANES), slice(None))

    for bi in range(NB):
        xs = [shifted(zs_ref[bi, :, tile(j)], car_s[bi, :, tile(j)], mu_s_ref[:, tile(j)])
              for j in range(GATE_BLOCK // LANES)]
        xg = [shifted(zg_ref[bi, :, tile(j)], car_g[bi, :, tile(j)], mu_g_ref[:, tile(j)])
              for j in range(GATE_BLOCK // LANES)]
        s_small = jnp.concatenate(xs, axis=1)
        s_gate = jnp.concatenate(xg, axis=1)
        lw = jnp.dot(jnp.tanh(s_small).astype(BF16), ww2_ref[...], preferred_element_type=F32)
        la = jnp.dot(s_small.astype(BF16), wa2_ref[...], preferred_element_type=F32)
        lg = jnp.dot(jax.nn.sigmoid(s_gate).astype(BF16), wg2_ref[...],
                     preferred_element_type=F32)
        for j in range(NT):
            LW[bi * NT + j] = lw[:, tile(j)]
            LA[bi * NT + j] = la[:, tile(j)]
            LG[bi * NT + j] = lg[:, tile(j)]
        car_g[bi] = zg_ref[bi, Tc - 1:Tc, :]
        car_s[bi] = zs_ref[bi, Tc - 1:Tc, :]

    for j in range(NT):
        for blk, dst in enumerate((R, K, None)):
            packed = _block_transpose([zr_ref[b, :, tile(blk * NT + j)] for b in range(NB)], RH)
            for g in range(NB):
                if dst is None:
                    VP[value_rows(NB * j + g)] = packed[g]
                else:
                    dst[NB * j + g] = packed[g]
        for src, dst in ((LW, W), (LA, BT)):
            packed = _block_transpose([src[b * NT + j] for b in range(NB)], RH)
            for g in range(NB):
                dst[NB * j + g] = packed[g]

    t_idx = lax.broadcasted_iota(jnp.int32, (Tc, LANES), 0)
    ss = jnp.zeros((Tc, LANES), F32)
    for n in range(N):
        raw = (R[n], K[n], VP[value_rows(n)])
        r, k, v = (shifted(raw[blk], car_r[blk * N + n:blk * N + n + 1, :],
                           mu_r_ref[blk * N + n:blk * N + n + 1, :]) for blk in range(3))
        for blk in range(3):
            car_r[blk * N + n:blk * N + n + 1, :] = raw[blk][Tc - 1:Tc, :]
        a = jax.nn.sigmoid(a0_ref[n:n + 1, :] + BT[n])
        kk = k * kkw_ref[n:n + 1, :]
        ss = ss + kk * kk
        log_w = -DECAY_SCALE * jax.nn.sigmoid(w0_ref[n:n + 1, :] + W[n])
        cum = log_w
        shift = 1
        while shift < Tc:
            cum = cum + jnp.where(t_idx >= shift, pltpu.roll(cum, shift, axis=0), 0.0)
            shift *= 2
        p_inv = jnp.exp(-cum)
        R[n] = r * jnp.exp(cum)
        VP[value_rows(n)] = v
        KK[n] = kk * jnp.exp(cum - log_w)
        K[n] = k * (1.0 + (a - 1.0) * ka_ref[n:n + 1, :]) * p_inv
        BT[n] = kk * a * p_inv
        PL[n:n + 1, :] = jnp.exp(cum[Tc - 1:Tc, :])
    inv = 1.0 / jnp.maximum(jnp.sqrt(ss), 1e-12)
    for n in range(N):
        KK[n] = KK[n] * inv
        BT[n] = BT[n] * inv

    def step(t, carry):
        row = pl.ds(t, 1)
        blk = pl.ds(pl.multiple_of(t * SUBLANES, SUBLANES), SUBLANES)
        acc = [[None, None] for _ in range(VH)]
        for key in range(N):
            kk_row = KK[key, row, :]
            for vh in range(VH):
                term = S[key, vh] * kk_row
                prev = acc[vh][key % 2]
                acc[vh][key % 2] = term if prev is None else prev + term
        sa = [-(acc[vh][0] + acc[vh][1]) for vh in range(VH)]
        vv = [VP[vh, blk, :] for vh in range(VH)]
        out = [[None, None] for _ in range(VH)]
        for key in range(N):
            b_row = BT[key, row, :]
            k_row = K[key, row, :]
            r_row = R[key, row, :]
            for vh in range(VH):
                s_new = S[key, vh] + sa[vh] * b_row + vv[vh] * k_row
                S[key, vh] = s_new
                term = s_new * r_row
                prev = out[vh][key % 2]
                out[vh][key % 2] = term if prev is None else prev + term
        for vh in range(VH):
            OP[vh, blk, :] = out[vh][0] + out[vh][1]
        return carry

    lax.fori_loop(0, Tc, step, 0, unroll=4)
    for key in range(N):
        for vh in range(VH):
            S[key, vh] = S[key, vh] * PL[key:key + 1, :]

    inv_n = 1.0 / N
    tot = jnp.zeros((Tc, LANES), F32)
    bon = jnp.zeros((Tc, LANES), F32)
    for n in range(N):
        tot = tot + OP[value_rows(n)]
        bon = bon + R[n] * K[n] * rk_ref[n:n + 1, :]
    mean = tot * inv_n
    var = jnp.zeros((Tc, LANES), F32)
    for n in range(N):
        d = OP[value_rows(n)] - mean
        var = var + d * d
    rstd = lax.rsqrt(var * inv_n + GN_EPS)
    for j in range(NT):
        ys = []
        for g in range(NB):
            n = NB * j + g
            o = (OP[value_rows(n)] - mean) * rstd * gnw_ref[n:n + 1, :] + gnb_ref[n:n + 1, :]
            ys.append(o + bon * VP[value_rows(n)])
        for b, y in enumerate(_block_transpose(ys, RH)):
            y_ref[b, :, tile(j)] = y * LG[b * NT + j]

    @pl.when(c == pl.num_programs(1) - 1)
    def _():
        s1_ref[...] = S[...]
        for b in range(NB):
            shn_r_ref[b] = zr_ref[b, Tc - 1:Tc, :]
        shn_g_ref[...] = car_g[...]
        shn_s_ref[...] = car_s[...]


def rwkv_branch(z, shp_r, sh_g, sh_s, S0, P, steps):
    B, T, _ = z.shape
    N = R_HEAD
    NB = RW_BATCH
    D = LANES // NB * N
    NT = D // LANES
    VH = N // SUBLANES
    assert B % NB == 0 and T % steps == 0 and steps % SUBLANES == 0
    g_blk = 8 * D // GATE_BLOCK

    def seq(width, j):
        return pl.BlockSpec((NB, steps, width), lambda g, c: (g, c, j))

    def per_batch(*shape):
        return pl.BlockSpec((NB,) + shape, lambda g, c: (g,) + (0,) * len(shape))

    def per_group(*shape):
        return pl.BlockSpec((None,) + shape, lambda g, c: (g,) + (0,) * len(shape))

    def const(*shape):
        return pl.BlockSpec(shape, lambda g, c: (0,) * len(shape))

    packed = pltpu.VMEM((N, steps, LANES), F32)
    natural = pltpu.VMEM((NB * NT, steps, LANES), F32)
    by_value = pltpu.VMEM((VH, steps * SUBLANES, LANES), F32)
    y, shn_r, shn_g, shn_s, S1 = pl.pallas_call(
        _rwkv_kernel,
        grid=(B // NB, T // steps),
        in_specs=[seq(3 * D, 1), seq(GATE_BLOCK, g_blk), seq(GATE_BLOCK, g_blk + 1),
                  per_group(3 * N, LANES), per_batch(1, GATE_BLOCK), per_batch(1, GATE_BLOCK),
                  const(3 * N, LANES), const(1, GATE_BLOCK), const(1, GATE_BLOCK),
                  const(GATE_BLOCK, D), const(GATE_BLOCK, D), const(GATE_BLOCK, D)]
                 + [const(N, LANES)] * 7
                 + [per_group(N, VH, SUBLANES, LANES)],
        out_specs=[seq(D, 0), per_batch(1, 3 * D), per_batch(1, GATE_BLOCK),
                   per_batch(1, GATE_BLOCK), per_group(N, VH, SUBLANES, LANES)],
        out_shape=[jax.ShapeDtypeStruct((B, T, D), F32),
                   jax.ShapeDtypeStruct((B, 1, 3 * D), F32),
                   jax.ShapeDtypeStruct((B, 1, GATE_BLOCK), F32),
                   jax.ShapeDtypeStruct((B, 1, GATE_BLOCK), F32),
                   jax.ShapeDtypeStruct((B // NB, N, VH, SUBLANES, LANES), F32)],
        scratch_shapes=[pltpu.VMEM((N, VH, SUBLANES, LANES), F32),
                        pltpu.VMEM((3 * N, LANES), F32), pltpu.VMEM((NB, 1, GATE_BLOCK), F32),
                        pltpu.VMEM((NB, 1, GATE_BLOCK), F32), pltpu.VMEM((N, LANES), F32)]
                       + [natural] * 3 + [packed] * 5 + [by_value] * 2,
        compiler_params=_params("parallel", "arbitrary"),
        name="rwkv7",
    )(z, z, z, shp_r, sh_g, sh_s, P["mu_r"], P["mu_g"], P["mu_s"],
      P["w_w2"], P["w_a2"], P["w_g2"], P["w0"], P["a0"], P["k_k"], P["k_a"], P["r_k"],
      P["gn_w"], P["gn_b"], S0)
    return y, shn_r, shn_g, shn_s, S1


def _nh(x):
    lead = x.shape[:-1]
    return x.reshape(lead + (-1, R_HEAD)).swapaxes(-1, -2).reshape(x.shape)


def _hn(x):
    lead = x.shape[:-1]
    return x.reshape(lead + (R_HEAD, -1)).swapaxes(-1, -2).reshape(x.shape)


def _lane_vec(x):
    return jnp.tile(x.reshape(-1, R_HEAD).T, (1, RW_BATCH))


def _pack_rows(x):
    B = x.shape[0]
    x = x.reshape(B // RW_BATCH, RW_BATCH, -1, R_HEAD)
    return x.transpose(0, 3, 1, 2).reshape(B // RW_BATCH, R_HEAD, LANES)


def _pack_state(S):
    B, RH = S.shape[:2]
    S = S.reshape(B // RW_BATCH, RW_BATCH, RH, R_HEAD // SUBLANES, SUBLANES, R_HEAD)
    return S.transpose(0, 5, 3, 4, 1, 2).reshape(B // RW_BATCH, R_HEAD, R_HEAD // SUBLANES,
                                                 SUBLANES, LANES)


def _unpack_state(S):
    G = S.shape[0]
    RH = LANES // RW_BATCH
    S = S.reshape(G, R_HEAD, R_HEAD // SUBLANES, SUBLANES, RW_BATCH, RH)
    return S.transpose(0, 4, 5, 2, 3, 1).reshape(G * RW_BATCH, RH, R_HEAD, R_HEAD)


def _pick_tile(n, candidates):
    for c in candidates:
        if n % c == 0:
            return c
    return n


def _repack_w_in(w, D, lora_w, lora_a, lora_g):
    H = M_HEADS
    o_end = 3 * D
    f_end = o_end + 2 * H
    zr = f_end
    xw = zr + 3 * D
    xa = xw + lora_w
    xg = xa + lora_a
    ga = xg + lora_g
    small = lora_w + lora_a + 2 * H
    assert lora_g == GATE_BLOCK and small <= GATE_BLOCK
    pad = jnp.zeros((w.shape[0], GATE_BLOCK - small), w.dtype)
    rkv = [_nh(w[:, zr + i * D:zr + (i + 1) * D]) for i in range(3)]
    return jnp.concatenate(
        [w[:, :o_end]] + rkv + [w[:, ga:ga + D], _nh(w[:, ga + D:ga + 2 * D]), w[:, xg:ga],
                                w[:, xw:xg], w[:, o_end:f_end], pad], axis=1)


def _layer(h, p_l, conv0, C0, n0, m0, shift0, S0, chunk, W, final, g_final):
    B, T, D = h.shape
    M = B * T
    tm = min(512, M)
    lora_w, lora_a, _ = W["lora"]
    gate_off = lora_w + lora_a

    h2 = h.reshape(M, D)
    z2 = norm_matmul(h2, W["g_mix"], W["w_in"], min(1024, M), W["w_in_tn"])
    z = z2.reshape(B, T, -1)

    y_a, conv_new, C1, n1, m1 = mlstm_branch(z, conv0, C0, n0, m0, W["conv_w"], W["conv_b"],
                                             W["gate_bias"], W["m_norm_g"], chunk, gate_off)

    shp_r = jnp.concatenate([_pack_rows(shift0[:, 0, i * D:(i + 1) * D]) for i in range(3)], axis=1)
    sh_g = shift0[..., 3 * D + gate_off:]
    sh_s = jnp.pad(shift0[..., 3 * D:3 * D + gate_off], ((0, 0), (0, 0), (0, GATE_BLOCK - gate_off)))
    y_b, shn_r, shn_g, shn_s, S1 = rwkv_branch(z, shp_r, sh_g, sh_s, _pack_state(S0), W["rwkv"],
                                               min(32, T))
    shift_new = jnp.concatenate([_hn(shn_r[..., i * D:(i + 1) * D]) for i in range(3)]
                                + [shn_s[..., :gate_off], shn_g], axis=-1)
    S1 = _unpack_state(S1)

    h2 = merge_project(h2, z2, y_a.reshape(M, D), y_b.reshape(M, D), W["w_out"], W["w_out_nh"],
                       min(256, M))
    h2 = ffn(h2, W["g_ffn"], W["w_gate"], W["w_up"], W["w_down"], tm, W["ffn_tf"])
    h2 = ple(h2, W["g_ple"], p_l.reshape(M, -1), W["w_ple_gate"], W["w_ple_in"], g_final, tm, final)
    return h2.reshape(B, T, D), (conv_new, C1, n1, m1, shift_new, S1)


def kernel(x_prompt, x_sample, p_prompt, p_sample, state_mlstm_C, state_mlstm_n, state_mlstm_m, state_mlstm_conv, state_rwkv_wkv, state_rwkv_shift, g_mix, w_in, conv_w, conv_b, b_i, b_f, m_norm_g, rwkv_mu, w0, w_w2, a0, w_a2, w_g2, k_k, k_a, r_k, gn_w, gn_b, w_out, g_ffn, w_gate, w_up, w_down, g_ple, w_ple_gate, w_ple_in, g_final):
    depth = w_in.shape[0]
    D = x_prompt.shape[-1]
    H = M_HEADS
    DV = D // H
    DK = DV // 2
    QK = H * DK
    RH = D // R_HEAD
    r_shift = state_rwkv_shift.shape[-1]
    lora = (w_w2.shape[1], w_a2.shape[1], w_g2.shape[1])
    n_pack = 8 * D + 2 * GATE_BLOCK
    w_in_tn = _pick_tile(n_pack, (768, 512, 256, 128))

    layers = []
    for l in range(depth):
        gate_bias = jnp.zeros((1, GATE_BLOCK), F32)
        gate_bias = gate_bias.at[0, lora[0] + lora[1]:lora[0] + lora[1] + 2 * H].set(
            jnp.concatenate([b_i[l], b_f[l]]))
        mu = rwkv_mu[l]
        lw, la = lora[0], lora[1]
        w_w2p = jnp.zeros((GATE_BLOCK, D), F32).at[:lw].set(_nh(w_w2[l]))
        w_a2p = jnp.zeros((GATE_BLOCK, D), F32).at[lw:lw + la].set(_nh(w_a2[l]))
        rwkv = dict(
            mu_r=jnp.concatenate([_lane_vec(mu[i * D:(i + 1) * D]) for i in range(3)]),
            mu_g=mu[3 * D + lw + la:][None],
            mu_s=jnp.pad(mu[3 * D:3 * D + lw + la], (0, GATE_BLOCK - lw - la))[None],
            w_w2=w_w2p.astype(BF16), w_a2=w_a2p.astype(BF16), w_g2=_nh(w_g2[l]).astype(BF16),
            w0=_lane_vec(w0[l]), a0=_lane_vec(a0[l]), k_k=_lane_vec(k_k[l]),
            k_a=_lane_vec(k_a[l]), r_k=_lane_vec(r_k[l].reshape(-1)),
            gn_w=_lane_vec(gn_w[l]), gn_b=_lane_vec(gn_b[l]))
        layers.append(dict(
            g_mix=g_mix[l][None], w_in=_repack_w_in(w_in[l], D, *lora).astype(BF16),
            w_in_tn=w_in_tn, lora=lora,
            conv_w=conv_w[l], conv_b=conv_b[l][None], gate_bias=gate_bias,
            m_norm_g=m_norm_g[l][None], rwkv=rwkv,
            w_out=w_out[l].astype(BF16),
            w_out_nh=_nh(w_out[l].T).T.astype(BF16), g_ffn=g_ffn[l][None],
            w_gate=w_gate[l].astype(BF16), w_up=w_up[l].astype(BF16),
            w_down=w_down[l].astype(BF16), ffn_tf=_pick_tile(w_gate.shape[2], (512, 256, 128)),
            g_ple=g_ple[l][None], w_ple_gate=w_ple_gate[l].astype(BF16),
            w_ple_in=w_ple_in[l].astype(BF16)))
    gf = g_final[None]

    def run(x, p, conv0, C0, n0, m0, shift0, S0, chunk):
        h = x
        outs = ([], [], [], [], [], [])
        for l in range(depth):
            h, st = _layer(h, p[l], conv0[l], C0[l], n0[l], m0[l], shift0[l], S0[l], chunk,
                           layers[l], l == depth - 1, gf)
            for lst, s in zip(outs, st):
                lst.append(s)
        return h, [jnp.stack(o, axis=0) for o in outs]

    Bp = x_prompt.shape[0]
    y_prompt, (conv_p, C_p, n_p, m_p, shift_p, wkv_p) = run(
        x_prompt, p_prompt,
        jnp.zeros((depth, Bp, CONV_W - 1, 2 * QK), F32),
        jnp.zeros((depth, Bp, H, DK, DV), F32),
        jnp.zeros((depth, Bp, H, DK), F32),
        jnp.zeros((depth, Bp, H), F32),
        jnp.zeros((depth, Bp, 1, r_shift), F32),
        jnp.zeros((depth, Bp, RH, R_HEAD, R_HEAD), F32),
        CHUNK)
    y_sample, (conv_s, C_s, n_s, m_s, shift_s, wkv_s) = run(
        x_sample, p_sample, state_mlstm_conv, state_mlstm_C, state_mlstm_n, state_mlstm_m,
        state_rwkv_shift, state_rwkv_wkv, x_sample.shape[1])
    return (y_prompt, y_sample, C_p, n_p, m_p, conv_p, wkv_p, shift_p,
            C_s, n_s, m_s, conv_s, wkv_s, shift_s)
```

```python
import functools

import jax
import jax.numpy as jnp
from jax import lax
from jax.experimental import pallas as pl
from jax.experimental.pallas import tpu as pltpu

F32 = jnp.float32
BF16 = jnp.bfloat16

EPS = 1e-6
GN_EPS = 64e-5
M_HEADS = 4
CONV_W = 4
R_HEAD = 64
CHUNK = 64
LANES = 128
SUBLANES = 8
VMEM_LIMIT = 48 * 1024 * 1024

GATE_BLOCK = 256


def _params(*sem):
    return pltpu.CompilerParams(dimension_semantics=sem, vmem_limit_bytes=VMEM_LIMIT)


def _rms(x, g):
    return x * lax.rsqrt(jnp.mean(x * x, axis=-1, keepdims=True) + EPS) * g


def _log_sigmoid(x):
    return jnp.minimum(x, 0.0) - jnp.log1p(jnp.exp(-jnp.abs(x)))


def _norm_mm_kernel(x_ref, g_ref, w_ref, o_ref, u_ref):
    @pl.when(pl.program_id(1) == 0)
    def _():
        u_ref[...] = _rms(x_ref[...], g_ref[...]).astype(BF16)

    o_ref[...] = jnp.dot(u_ref[...], w_ref[...], preferred_element_type=F32)


def norm_matmul(x, g, w, tm, tn):
    M, K = x.shape
    N = w.shape[1]
    return pl.pallas_call(
        _norm_mm_kernel,
        grid=(M // tm, N // tn),
        in_specs=[pl.BlockSpec((tm, K), lambda i, j: (i, 0)),
                  pl.BlockSpec((1, K), lambda i, j: (0, 0)),
                  pl.BlockSpec((K, tn), lambda i, j: (0, j))],
        out_specs=pl.BlockSpec((tm, tn), lambda i, j: (i, j)),
        out_shape=jax.ShapeDtypeStruct((M, N), F32),
        scratch_shapes=[pltpu.VMEM((tm, K), BF16)],
        compiler_params=_params("parallel", "arbitrary"),
        name="norm_matmul",
    )(x, g, w)


def _merge_kernel(h_ref, ga_ref, gb_ref, ya_ref, yb_ref, wa_ref, wb_ref, o_ref):
    ma = (jax.nn.sigmoid(ga_ref[...]) * ya_ref[...]).astype(BF16)
    mb = (jax.nn.sigmoid(gb_ref[...]) * yb_ref[...]).astype(BF16)
    o_ref[...] = (h_ref[...] + jnp.dot(ma, wa_ref[...], preferred_element_type=F32)
                  + jnp.dot(mb, wb_ref[...], preferred_element_type=F32))


def merge_project(h, z, ya, yb, wa, wb, tm):
    M, D = h.shape
    row = pl.BlockSpec((tm, D), lambda i: (i, 0))
    wspec = pl.BlockSpec((D, D), lambda i: (0, 0), pipeline_mode=pl.Buffered(1))
    return pl.pallas_call(
        _merge_kernel,
        grid=(M // tm,),
        in_specs=[row,
                  pl.BlockSpec((tm, D), lambda i: (i, 6)),
                  pl.BlockSpec((tm, D), lambda i: (i, 7)),
                  row, row, wspec, wspec],
        out_specs=row,
        out_shape=jax.ShapeDtypeStruct((M, D), F32),
        compiler_params=_params("parallel"),
        name="merge_project",
    )(h, z, z, ya, yb, wa, wb)


def _ffn_kernel(h_ref, g_ref, wg_ref, wu_ref, wd_ref, o_ref, u_ref, acc_ref):
    f = pl.program_id(1)

    @pl.when(f == 0)
    def _():
        u_ref[...] = _rms(h_ref[...], g_ref[...]).astype(BF16)
        acc_ref[...] = jnp.zeros_like(acc_ref)

    u = u_ref[...]
    gate = jnp.dot(u, wg_ref[...], preferred_element_type=F32)
    up = jnp.dot(u, wu_ref[...], preferred_element_type=F32)
    act = (gate * jax.nn.sigmoid(gate) * up).astype(BF16)
    acc_ref[...] += jnp.dot(act, wd_ref[...], preferred_element_type=F32)

    @pl.when(f == pl.num_programs(1) - 1)
    def _():
        o_ref[...] = h_ref[...] + acc_ref[...]


def ffn(h, g, wg, wu, wd, tm, tf):
    M, D = h.shape
    F = wg.shape[1]
    return pl.pallas_call(
        _ffn_kernel,
        grid=(M // tm, F // tf),
        in_specs=[pl.BlockSpec((tm, D), lambda i, j: (i, 0)),
                  pl.BlockSpec((1, D), lambda i, j: (0, 0)),
                  pl.BlockSpec((D, tf), lambda i, j: (0, j)),
                  pl.BlockSpec((D, tf), lambda i, j: (0, j)),
                  pl.BlockSpec((tf, D), lambda i, j: (j, 0))],
        out_specs=pl.BlockSpec((tm, D), lambda i, j: (i, 0)),
        out_shape=jax.ShapeDtypeStruct((M, D), F32),
        scratch_shapes=[pltpu.VMEM((tm, D), BF16), pltpu.VMEM((tm, D), F32)],
        compiler_params=_params("parallel", "arbitrary"),
        name="ffn",
    )(h, g, wg, wu, wd)


def _ple_kernel(h_ref, g_ref, p_ref, wpg_ref, wpi_ref, gf_ref, o_ref, *, final):
    h = h_ref[...]
    u = _rms(h, g_ref[...]).astype(BF16)
    gate = jnp.dot(u, wpg_ref[...], preferred_element_type=F32)
    emb = jnp.dot(p_ref[...].astype(BF16), wpi_ref[...], preferred_element_type=F32)
    out = h + jax.nn.sigmoid(gate) * emb
    if final:
        out = _rms(out, gf_ref[...])
    o_ref[...] = out


def ple(h, g, p, wpg, wpi, g_final, tm, final):
    M, D = h.shape
    P = p.shape[1]
    return pl.pallas_call(
        functools.partial(_ple_kernel, final=final),
        grid=(M // tm,),
        in_specs=[pl.BlockSpec((tm, D), lambda i: (i, 0)),
                  pl.BlockSpec((1, D), lambda i: (0, 0)),
                  pl.BlockSpec((tm, P), lambda i: (i, 0)),
                  pl.BlockSpec((D, D), lambda i: (0, 0), pipeline_mode=pl.Buffered(1)),
                  pl.BlockSpec((P, D), lambda i: (0, 0), pipeline_mode=pl.Buffered(1)),
                  pl.BlockSpec((1, D), lambda i: (0, 0))],
        out_specs=pl.BlockSpec((tm, D), lambda i: (i, 0)),
        out_shape=jax.ShapeDtypeStruct((M, D), F32),
        compiler_params=_params("parallel"),
        name="ple",
    )(h, g, p, wpg, wpi, g_final)


def _mlstm_kernel(qk_ref, v_ref, o_ref, gt_ref, conv0_ref, cw_ref, cb_ref, gbias_ref, mg_ref,
                  c0_ref, n0_ref, m0_ref,
                  y_ref, convn_ref, c1_ref, n1_ref, m1_ref,
                  xp, C, n, m, *, gate_off):
    c = pl.program_id(1)
    L, D = qk_ref.shape
    H = M_HEADS
    DV = D // H
    DK = DV // 2
    QK = H * DK
    halo = CONV_W - 1
    top = SUBLANES - halo

    @pl.when(c == 0)
    def _():
        xp[top:SUBLANES, :] = conv0_ref[...]
        C[...] = c0_ref[...]
        n[...] = n0_ref[...]
        m[...] = m0_ref[...]

    xp[SUBLANES:SUBLANES + L, :] = qk_ref[...]
    gt = gt_ref[...] + gbias_ref[...]

    row = lax.broadcasted_iota(jnp.int32, (L, L), 0)
    col = lax.broadcasted_iota(jnp.int32, (L, L), 1)
    tril = row >= col
    eye = row == col

    def conv_silu(c0):
        acc = cb_ref[:, c0:c0 + DK] + xp[top:top + L, c0:c0 + DK] * cw_ref[0:1, c0:c0 + DK]
        for j in range(1, CONV_W):
            acc = acc + xp[top + j:top + j + L, c0:c0 + DK] * cw_ref[j:j + 1, c0:c0 + DK]
        return acc * jax.nn.sigmoid(acc)

    for h in range(H):
        q = conv_silu(h * DK)
        k = conv_silu(QK + h * DK) * (DK ** -0.5)
        vb = v_ref[:, h * DV:(h + 1) * DV].astype(BF16)
        li_c = gt[:, gate_off + h:gate_off + h + 1]
        lf_c = _log_sigmoid(gt[:, gate_off + H + h:gate_off + H + h + 1])
        li_r = jnp.sum(jnp.where(eye, li_c, 0.0), axis=0, keepdims=True)
        b_row = jnp.sum(jnp.where(row <= col, lf_c, 0.0), axis=0, keepdims=True)
        b_col = jnp.sum(jnp.where(eye, b_row, 0.0), axis=1, keepdims=True)
        m_old = m[h]
        a_col = b_col + m_old
        dmat = jnp.where(tril, b_col - b_row + li_r, -jnp.inf)
        mt = jnp.maximum(a_col, jnp.max(dmat, axis=1, keepdims=True))
        ea = jnp.exp(a_col - mt)
        p = jnp.exp(dmat - mt)

        qb = q.astype(BF16)
        Ch = C[h]
        nh = n[h]
        s = lax.dot_general(qb, k.astype(BF16), (((1,), (1,)), ((), ())),
                            preferred_element_type=F32) * p
        num = (ea * jnp.dot(qb, Ch.astype(BF16), preferred_element_type=F32)
               + jnp.dot(s.astype(BF16), vb, preferred_element_type=F32))
        den = ea * jnp.sum(q * nh, axis=1, keepdims=True) + jnp.sum(s, axis=1, keepdims=True)
        hh = num / jnp.maximum(jnp.abs(den), jnp.exp(-mt))
        hh = hh * lax.rsqrt(jnp.mean(hh * hh, axis=1, keepdims=True) + EPS)
        y_ref[:, h * DV:(h + 1) * DV] = (jax.nn.sigmoid(o_ref[:, h * DV:(h + 1) * DV])
                                         * (hh * mg_ref[:, h * DV:(h + 1) * DV]))

        m_new = mt[L - 1:L, :]
        dec = jnp.exp(a_col[L - 1:L, :] - m_new)
        wk = jnp.exp(b_col[L - 1:L, :] - b_col + li_c - m_new)
        kw = k * wk
        C[h] = dec * Ch + lax.dot_general(kw.astype(BF16), vb, (((0,), (0,)), ((), ())),
                                          preferred_element_type=F32)
        n[h] = dec * nh + jnp.sum(kw, axis=0, keepdims=True)
        m[h] = m_new

    xp[top:SUBLANES, :] = xp[top + L:SUBLANES + L, :]

    @pl.when(c == pl.num_programs(1) - 1)
    def _():
        convn_ref[...] = xp[top:SUBLANES, :]
        c1_ref[...] = C[...]
        n1_ref[...] = n[...]
        m1_ref[...] = m[...]


def mlstm_branch(z, conv0, C0, n0, m0, conv_w, conv_b, gate_bias, m_norm_g, L, gate_off):
    B, T, _ = z.shape
    H = M_HEADS
    _, _, DK, DV = C0.shape
    D = H * DV
    nc = T // L
    halo = CONV_W - 1

    def seq(j):
        return pl.BlockSpec((None, L, D), lambda b, c: (b, c, j))

    def per_batch(*shape):
        return pl.BlockSpec((None,) + shape, lambda b, c: (b,) + (0,) * len(shape))

    def const(*shape):
        return pl.BlockSpec(shape, lambda b, c: (0,) * len(shape))

    gate_blk = (8 * D + GATE_BLOCK) // GATE_BLOCK
    y, conv_new, C1, n1, m1 = pl.pallas_call(
        functools.partial(_mlstm_kernel, gate_off=gate_off),
        grid=(B, nc),
        in_specs=[seq(0), seq(1), seq(2),
                  pl.BlockSpec((None, L, GATE_BLOCK), lambda b, c: (b, c, gate_blk)),
                  per_batch(halo, D), const(CONV_W, D), const(1, D), const(1, GATE_BLOCK),
                  const(1, D),
                  per_batch(H, DK, DV), per_batch(H, 1, DK), per_batch(H, 1, 1)],
        out_specs=[seq(0), per_batch(halo, D),
                   per_batch(H, DK, DV), per_batch(H, 1, DK), per_batch(H, 1, 1)],
        out_shape=[jax.ShapeDtypeStruct((B, T, D), F32),
                   jax.ShapeDtypeStruct((B, halo, D), F32),
                   jax.ShapeDtypeStruct((B, H, DK, DV), F32),
                   jax.ShapeDtypeStruct((B, H, 1, DK), F32),
                   jax.ShapeDtypeStruct((B, H, 1, 1), F32)],
        scratch_shapes=[pltpu.VMEM((L + SUBLANES, D), F32), pltpu.VMEM((H, DK, DV), F32),
                        pltpu.VMEM((H, 1, DK), F32), pltpu.VMEM((H, 1, 1), F32)],
        compiler_params=_params("parallel", "arbitrary"),
        name="mlstm",
    )(z, z, z, z, conv0, conv_w, conv_b, gate_bias, m_norm_g,
      C0, n0[:, :, None, :], m0[:, :, None, None])
    return y, conv_new, C1, n1[:, :, 0, :], m1[:, :, 0, 0]


RW_BATCH = 4
DECAY_SCALE = 0.6065306597126334


def _block_transpose(a, rh):
    blk = lax.broadcasted_iota(jnp.int32, a[0].shape, 1) // rh
    out = []
    for g in range(4):
        acc = None
        for b in range(4):
            src = a[b] if b == g else pltpu.roll(a[b], ((b - g) % 4) * rh, axis=1)
            acc = src if acc is None else jnp.where(blk == b, src, acc)
        out.append(acc)
    return tuple(out)


def _rwkv_kernel(zr_ref, zg_ref, zs_ref, shp_r_ref, sh_g_ref, sh_s_ref,
                 mu_r_ref, mu_g_ref, mu_s_ref, ww2_ref, wa2_ref, wg2_ref,
                 w0_ref, a0_ref, kkw_ref, ka_ref, rk_ref, gnw_ref, gnb_ref, s0_ref,
                 y_ref, shn_r_ref, shn_g_ref, shn_s_ref, s1_ref,
                 S, car_r, car_g, car_s, PL, LW, LA, LG, R, W, K, KK, BT, VP, OP):
    c = pl.program_id(1)
    NB, Tc, D3 = zr_ref.shape
    D = D3 // 3
    RH = D // R_HEAD
    N = R_HEAD
    NT = D // LANES
    VH = N // SUBLANES

    @pl.when(c == 0)
    def _():
        S[...] = s0_ref[...]
        car_r[...] = shp_r_ref[...]
        car_g[...] = sh_g_ref[...]
        car_s[...] = sh_s_ref[...]

    first_row = lax.broadcasted_iota(jnp.int32, (Tc, LANES), 0) == 0

    def shifted(x, carry_row, mu):
        prev = jnp.where(first_row, carry_row, pltpu.roll(x, 1, axis=0))
        return x + mu * (prev - x)

    def tile(j):
        return slice(j * LANES, (j + 1) * LANES)

    def value_rows(n):
        return (n // SUBLANES, pl.ds(n % SUBLANES, Tc, stride=SUBLANES), slice(None))

    for bi in range(NB):
        xs = [shifted(zs_ref[bi, :, tile(j)], car_s[bi, :, tile(j)], mu_s_ref[:, tile(j)])
              for j in range(GATE_BLOCK // LANES)]
        xg = [shifted(zg_ref[bi, :, tile(j)], car_g[bi, :, tile(j)], mu_g_ref[:, tile(j)])
              for j in range(GATE_BLOCK // LANES)]
        s_small = jnp.concatenate(xs, axis=1)
        s_gate = jnp.concatenate(xg, axis=1)
        lw = jnp.dot(jnp.tanh(s_small).astype(BF16), ww2_ref[...], preferred_element_type=F32)
        la = jnp.dot(s_small.astype(BF16), wa2_ref[...], preferred_element_type=F32)
        lg = jnp.dot(jax.nn.sigmoid(s_gate).astype(BF16), wg2_ref[...],
                     preferred_element_type=F32)
        for j in range(NT):
            LW[bi * NT + j] = lw[:, tile(j)]
            LA[bi * NT + j] = la[:, tile(j)]
            LG[bi * NT + j] = lg[:, tile(j)]
        car_g[bi] = zg_ref[bi, Tc - 1:Tc, :]
        car_s[bi] = zs_ref[bi, Tc - 1:Tc, :]

    for j in range(NT):
        for blk, dst in enumerate((R, K, None)):
            packed = _block_transpose([zr_ref[b, :, tile(blk * NT + j)] for b in range(NB)], RH)
            for g in range(NB):
                if dst is None:
                    VP[value_rows(NB * j + g)] = packed[g]
                else:
                    dst[NB * j + g] = packed[g]
        for src, dst in ((LW, W), (LA, BT)):
            packed = _block_transpose([src[b * NT + j] for b in range(NB)], RH)
            for g in range(NB):
                dst[NB * j + g] = packed[g]

    t_idx = lax.broadcasted_iota(jnp.int32, (Tc, LANES), 0)
    ss = jnp.zeros((Tc, LANES), F32)
    for n in range(N):
        raw = (R[n], K[n], VP[value_rows(n)])
        r, k, v = (shifted(raw[blk], car_r[blk * N + n:blk * N + n + 1, :],
                           mu_r_ref[blk * N + n:blk * N + n + 1, :]) for blk in range(3))
        for blk in range(3):
            car_r[blk * N + n:blk * N + n + 1, :] = raw[blk][Tc - 1:Tc, :]
        a = jax.nn.sigmoid(a0_ref[n:n + 1, :] + BT[n])
        kk = k * kkw_ref[n:n + 1, :]
        ss = ss + kk * kk
        log_w = -DECAY_SCALE * jax.nn.sigmoid(w0_ref[n:n + 1, :] + W[n])
        cum = log_w
        shift = 1
        while shift < Tc:
            cum = cum + jnp.where(t_idx >= shift, pltpu.roll(cum, shift, axis=0), 0.0)
            shift *= 2
        p_inv = jnp.exp(-cum)
        R[n] = r * jnp.exp(cum)
        VP[value_rows(n)] = v
        KK[n] = kk * jnp.exp(cum - log_w)
        K[n] = k * (1.0 + (a - 1.0) * ka_ref[n:n + 1, :]) * p_inv
        BT[n] = kk * a * p_inv
        PL[n:n + 1, :] = jnp.exp(cum[Tc - 1:Tc, :])
    inv = 1.0 / jnp.maximum(jnp.sqrt(ss), 1e-12)
    for n in range(N):
        KK[n] = KK[n] * inv
        BT[n] = BT[n] * inv

    def step(t, carry):
        row = pl.ds(t, 1)
        blk = pl.ds(pl.multiple_of(t * SUBLANES, SUBLANES), SUBLANES)
        acc = [[None, None] for _ in range(VH)]
        for key in range(N):
            kk_row = KK[key, row, :]
            for vh in range(VH):
                term = S[key, vh] * kk_row
                prev = acc[vh][key % 2]
                acc[vh][key % 2] = term if prev is None else prev + term
        sa = [-(acc[vh][0] + acc[vh][1]) for vh in range(VH)]
        vv = [VP[vh, blk, :] for vh in range(VH)]
        out = [[None, None] for _ in range(VH)]
        for key in range(N):
            b_row = BT[key, row, :]
            k_row = K[key, row, :]
            r_row = R[key, row, :]
            for vh in range(VH):
                s_new = S[key, vh] + sa[vh] * b_row + vv[vh] * k_row
                S[key, vh] = s_new
                term = s_new * r_row
                prev = out[vh][key % 2]
                out[vh][key % 2] = term if prev is None else prev + term
        for vh in range(VH):
            OP[vh, blk, :] = out[vh][0] + out[vh][1]
        return carry

    lax.fori_loop(0, Tc, step, 0, unroll=8)
    for key in range(N):
        for vh in range(VH):
            S[key, vh] = S[key, vh] * PL[key:key + 1, :]

    inv_n = 1.0 / N
    tot = jnp.zeros((Tc, LANES), F32)
    bon = jnp.zeros((Tc, LANES), F32)
    for n in range(N):
        tot = tot + OP[value_rows(n)]
        bon = bon + R[n] * K[n] * rk_ref[n:n + 1, :]
    mean = tot * inv_n
    var = jnp.zeros((Tc, LANES), F32)
    for n in range(N):
        d = OP[value_rows(n)] - mean
        var = var + d * d
    rstd = lax.rsqrt(var * inv_n + GN_EPS)
    for j in range(NT):
        ys = []
        for g in range(NB):
            n = NB * j + g
            o = (OP[value_rows(n)] - mean) * rstd * gnw_ref[n:n + 1, :] + gnb_ref[n:n + 1, :]
            ys.append(o + bon * VP[value_rows(n)])
        for b, y in enumerate(_block_transpose(ys, RH)):
            y_ref[b, :, tile(j)] = y * LG[b * NT + j]

    @pl.when(c == pl.num_programs(1) - 1)
    def _():
        s1_ref[...] = S[...]
        for b in range(NB):
            shn_r_ref[b] = zr_ref[b, Tc - 1:Tc, :]
        shn_g_ref[...] = car_g[...]
        shn_s_ref[...] = car_s[...]


def rwkv_branch(z, shp_r, sh_g, sh_s, S0, P, steps):
    B, T, _ = z.shape
    N = R_HEAD
    NB = RW_BATCH
    D = LANES // NB * N
    NT = D // LANES
    VH = N // SUBLANES
    assert B % NB == 0 and T % steps == 0 and steps % SUBLANES == 0
    g_blk = 8 * D // GATE_BLOCK

    def seq(width, j):
        return pl.BlockSpec((NB, steps, width), lambda g, c: (g, c, j))

    def per_batch(*shape):
        return pl.BlockSpec((NB,) + shape, lambda g, c: (g,) + (0,) * len(shape))

    def per_group(*shape):
        return pl.BlockSpec((None,) + shape, lambda g, c: (g,) + (0,) * len(shape))

    def const(*shape):
        return pl.BlockSpec(shape, lambda g, c: (0,) * len(shape))

    packed = pltpu.VMEM((N, steps, LANES), F32)
    natural = pltpu.VMEM((NB * NT, steps, LANES), F32)
    by_value = pltpu.VMEM((VH, steps * SUBLANES, LANES), F32)
    y, shn_r, shn_g, shn_s, S1 = pl.pallas_call(
        _rwkv_kernel,
        grid=(B // NB, T // steps),
        in_specs=[seq(3 * D, 1), seq(GATE_BLOCK, g_blk), seq(GATE_BLOCK, g_blk + 1),
                  per_group(3 * N, LANES), per_batch(1, GATE_BLOCK), per_batch(1, GATE_BLOCK),
                  const(3 * N, LANES), const(1, GATE_BLOCK), const(1, GATE_BLOCK),
                  const(GATE_BLOCK, D), const(GATE_BLOCK, D), const(GATE_BLOCK, D)]
                 + [const(N, LANES)] * 7
                 + [per_group(N, VH, SUBLANES, LANES)],
        out_specs=[seq(D, 0), per_batch(1, 3 * D), per_batch(1, GATE_BLOCK),
                   per_batch(1, GATE_BLOCK), per_group(N, VH, SUBLANES, LANES)],
        out_shape=[jax.ShapeDtypeStruct((B, T, D), F32),
                   jax.ShapeDtypeStruct((B, 1, 3 * D), F32),
                   jax.ShapeDtypeStruct((B, 1, GATE_BLOCK), F32),
                   jax.ShapeDtypeStruct((B, 1, GATE_BLOCK), F32),
                   jax.ShapeDtypeStruct((B // NB, N, VH, SUBLANES, LANES), F32)],
        scratch_shapes=[pltpu.VMEM((N, VH, SUBLANES, LANES), F32),
                        pltpu.VMEM((3 * N, LANES), F32), pltpu.VMEM((NB, 1, GATE_BLOCK), F32),
                        pltpu.VMEM((NB, 1, GATE_BLOCK), F32), pltpu.VMEM((N, LANES), F32)]
                       + [natural] * 3 + [packed] * 5 + [by_value] * 2,
        compiler_params=_params("parallel", "arbitrary"),
        name="rwkv7",
    )(z, z, z, shp_r, sh_g, sh_s, P["mu_r"], P["mu_g"], P["mu_s"],
      P["w_w2"], P["w_a2"], P["w_g2"], P["w0"], P["a0"], P["k_k"], P["k_a"], P["r_k"],
      P["gn_w"], P["gn_b"], S0)
    return y, shn_r, shn_g, shn_s, S1


def _nh(x):
    lead = x.shape[:-1]
    return x.reshape(lead + (-1, R_HEAD)).swapaxes(-1, -2).reshape(x.shape)


def _hn(x):
    lead = x.shape[:-1]
    return x.reshape(lead + (R_HEAD, -1)).swapaxes(-1, -2).reshape(x.shape)


def _lane_vec(x):
    return jnp.tile(x.reshape(-1, R_HEAD).T, (1, RW_BATCH))


def _pack_rows(x):
    B = x.shape[0]
    x = x.reshape(B // RW_BATCH, RW_BATCH, -1, R_HEAD)
    return x.transpose(0, 3, 1, 2).reshape(B // RW_BATCH, R_HEAD, LANES)


def _pack_state(S):
    B, RH = S.shape[:2]
    S = S.reshape(B // RW_BATCH, RW_BATCH, RH, R_HEAD // SUBLANES, SUBLANES, R_HEAD)
    return S.transpose(0, 5, 3, 4, 1, 2).reshape(B // RW_BATCH, R_HEAD, R_HEAD // SUBLANES,
                                                 SUBLANES, LANES)


def _unpack_state(S):
    G = S.shape[0]
    RH = LANES // RW_BATCH
    S = S.reshape(G, R_HEAD, R_HEAD // SUBLANES, SUBLANES, RW_BATCH, RH)
    return S.transpose(0, 4, 5, 2, 3, 1).reshape(G * RW_BATCH, RH, R_HEAD, R_HEAD)


def _pick_tile(n, candidates):
    for c in candidates:
        if n % c == 0:
            return c
    return n


def _repack_w_in(w, D, lora_w, lora_a, lora_g):
    H = M_HEADS
    o_end = 3 * D
    f_end = o_end + 2 * H
    zr = f_end
    xw = zr + 3 * D
    xa = xw + lora_w
    xg = xa + lora_a
    ga = xg + lora_g
    small = lora_w + lora_a + 2 * H
    assert lora_g == GATE_BLOCK and small <= GATE_BLOCK
    pad = jnp.zeros((w.shape[0], GATE_BLOCK - small), w.dtype)
    rkv = [_nh(w[:, zr + i * D:zr + (i + 1) * D]) for i in range(3)]
    return jnp.concatenate(
        [w[:, :o_end]] + rkv + [w[:, ga:ga + D], _nh(w[:, ga + D:ga + 2 * D]), w[:, xg:ga],
                                w[:, xw:xg], w[:, o_end:f_end], pad], axis=1)


def _layer(h, p_l, conv0, C0, n0, m0, shift0, S0, chunk, W, final, g_final):
    B, T, D = h.shape
    M = B * T
    tm = min(512, M)
    lora_w, lora_a, _ = W["lora"]
    gate_off = lora_w + lora_a

    h2 = h.reshape(M, D)
    z2 = norm_matmul(h2, W["g_mix"], W["w_in"], min(1024, M), W["w_in_tn"])
    z = z2.reshape(B, T, -1)

    y_a, conv_new, C1, n1, m1 = mlstm_branch(z, conv0, C0, n0, m0, W["conv_w"], W["conv_b"],
                                             W["gate_bias"], W["m_norm_g"], chunk, gate_off)

    shp_r = jnp.concatenate([_pack_rows(shift0[:, 0, i * D:(i + 1) * D]) for i in range(3)], axis=1)
    sh_g = shift0[..., 3 * D + gate_off:]
    sh_s = jnp.pad(shift0[..., 3 * D:3 * D + gate_off], ((0, 0), (0, 0), (0, GATE_BLOCK - gate_off)))
    y_b, shn_r, shn_g, shn_s, S1 = rwkv_branch(z, shp_r, sh_g, sh_s, _pack_state(S0), W["rwkv"],
                                               min(32, T))
    shift_new = jnp.concatenate([_hn(shn_r[..., i * D:(i + 1) * D]) for i in range(3)]
                                + [shn_s[..., :gate_off], shn_g], axis=-1)
    S1 = _unpack_state(S1)

    h2 = merge_project(h2, z2, y_a.reshape(M, D), y_b.reshape(M, D), W["w_out"], W["w_out_nh"],
                       min(256, M))
    h2 = ffn(h2, W["g_ffn"], W["w_gate"], W["w_up"], W["w_down"], tm, W["ffn_tf"])
    h2 = ple(h2, W["g_ple"], p_l.reshape(M, -1), W["w_ple_gate"], W["w_ple_in"], g_final, tm, final)
    return h2.reshape(B, T, D), (conv_new, C1, n1, m1, shift_new, S1)


def kernel(x_prompt, x_sample, p_prompt, p_sample, state_mlstm_C, state_mlstm_n, state_mlstm_m, state_mlstm_conv, state_rwkv_wkv, state_rwkv_shift, g_mix, w_in, conv_w, conv_b, b_i, b_f, m_norm_g, rwkv_mu, w0, w_w2, a0, w_a2, w_g2, k_k, k_a, r_k, gn_w, gn_b, w_out, g_ffn, w_gate, w_up, w_down, g_ple, w_ple_gate, w_ple_in, g_final):
    depth = w_in.shape[0]
    D = x_prompt.shape[-1]
    H = M_HEADS
    DV = D // H
    DK = DV // 2
    QK = H * DK
    RH = D // R_HEAD
    r_shift = state_rwkv_shift.shape[-1]
    lora = (w_w2.shape[1], w_a2.shape[1], w_g2.shape[1])
    n_pack = 8 * D + 2 * GATE_BLOCK
    w_in_tn = _pick_tile(n_pack, (1536, 768, 512, 256, 128))

    layers = []
    for l in range(depth):
        gate_bias = jnp.zeros((1, GATE_BLOCK), F32)
        gate_bias = gate_bias.at[0, lora[0] + lora[1]:lora[0] + lora[1] + 2 * H].set(
            jnp.concatenate([b_i[l], b_f[l]]))
        mu = rwkv_mu[l]
        lw, la = lora[0], lora[1]
        w_w2p = jnp.zeros((GATE_BLOCK, D), F32).at[:lw].set(_nh(w_w2[l]))
        w_a2p = jnp.zeros((GATE_BLOCK, D), F32).at[lw:lw + la].set(_nh(w_a2[l]))
        rwkv = dict(
            mu_r=jnp.concatenate([_lane_vec(mu[i * D:(i + 1) * D]) for i in range(3)]),
            mu_g=mu[3 * D + lw + la:][None],
            mu_s=jnp.pad(mu[3 * D:3 * D + lw + la], (0, GATE_BLOCK - lw - la))[None],
            w_w2=w_w2p.astype(BF16), w_a2=w_a2p.astype(BF16), w_g2=_nh(w_g2[l]).astype(BF16),
            w0=_lane_vec(w0[l]), a0=_lane_vec(a0[l]), k_k=_lane_vec(k_k[l]),
            k_a=_lane_vec(k_a[l]), r_k=_lane_vec(r_k[l].reshape(-1)),
            gn_w=_lane_vec(gn_w[l]), gn_b=_lane_vec(gn_b[l]))
        layers.append(dict(
            g_mix=g_mix[l][None], w_in=_repack_w_in(w_in[l], D, *lora).astype(BF16),
            w_in_tn=w_in_tn, lora=lora,
            conv_w=conv_w[l], conv_b=conv_b[l][None], gate_bias=gate_bias,
            m_norm_g=m_norm_g[l][None], rwkv=rwkv,
            w_out=w_out[l].astype(BF16),
            w_out_nh=_nh(w_out[l].T).T.astype(BF16), g_ffn=g_ffn[l][None],
            w_gate=w_gate[l].astype(BF16), w_up=w_up[l].astype(BF16),
            w_down=w_down[l].astype(BF16), ffn_tf=_pick_tile(w_gate.shape[2], (512, 256, 128)),
            g_ple=g_ple[l][None], w_ple_gate=w_ple_gate[l].astype(BF16),
            w_ple_in=w_ple_in[l].astype(BF16)))
    gf = g_final[None]

    def run(x, p, conv0, C0, n0, m0, shift0, S0, chunk):
        h = x
        outs = ([], [], [], [], [], [])
        for l in range(depth):
            h, st = _layer(h, p[l], conv0[l], C0[l], n0[l], m0[l], shift0[l], S0[l], chunk,
                           layers[l], l == depth - 1, gf)
            for lst, s in zip(outs, st):
                lst.append(s)
        return h, [jnp.stack(o, axis=0) for o in outs]

    Bp = x_prompt.shape[0]
    y_prompt, (conv_p, C_p, n_p, m_p, shift_p, wkv_p) = run(
        x_prompt, p_prompt,
        jnp.zeros((depth, Bp, CONV_W - 1, 2 * QK), F32),
        jnp.zeros((depth, Bp, H, DK, DV), F32),
        jnp.zeros((depth, Bp, H, DK), F32),
        jnp.zeros((depth, Bp, H), F32),
        jnp.zeros((depth, Bp, 1, r_shift), F32),
        jnp.zeros((depth, Bp, RH, R_HEAD, R_HEAD), F32),
        CHUNK)
    y_sample, (conv_s, C_s, n_s, m_s, shift_s, wkv_s) = run(
        x_sample, p_sample, state_mlstm_conv, state_mlstm_C, state_mlstm_n, state_mlstm_m,
        state_rwkv_shift, state_rwkv_wkv, x_sample.shape[1])
    return (y_prompt, y_sample, C_p, n_p, m_p, conv_p, wkv_p, shift_p,
            C_s, n_s, m_s, conv_s, wkv_s, shift_s)
```

```python
import functools

import jax
import jax.numpy as jnp
from jax import lax
from jax.experimental import pallas as pl
from jax.experimental.pallas import tpu as pltpu

F32 = jnp.float32
BF16 = jnp.bfloat16

EPS = 1e-6
GN_EPS = 64e-5
M_HEADS = 4
CONV_W = 4
R_HEAD = 64
CHUNK = 64
MLSTM_CHUNKS_PER_STEP = 4
LANES = 128
SUBLANES = 8
VMEM_LIMIT = 48 * 1024 * 1024

GATE_BLOCK = 256


def _params(*sem):
    return pltpu.CompilerParams(dimension_semantics=sem, vmem_limit_bytes=VMEM_LIMIT)


def _rms(x, g):
    return x * lax.rsqrt(jnp.mean(x * x, axis=-1, keepdims=True) + EPS) * g


def _log_sigmoid(x):
    return jnp.minimum(x, 0.0) - jnp.log1p(jnp.exp(-jnp.abs(x)))


def _norm_mm_kernel(x_ref, g_ref, w_ref, o_ref, u_ref):
    @pl.when(pl.program_id(1) == 0)
    def _():
        u_ref[...] = _rms(x_ref[...], g_ref[...]).astype(BF16)

    o_ref[...] = jnp.dot(u_ref[...], w_ref[...], preferred_element_type=F32)


def norm_matmul(x, g, w, tm, tn):
    M, K = x.shape
    N = w.shape[1]
    return pl.pallas_call(
        _norm_mm_kernel,
        grid=(M // tm, N // tn),
        in_specs=[pl.BlockSpec((tm, K), lambda i, j: (i, 0)),
                  pl.BlockSpec((1, K), lambda i, j: (0, 0)),
                  pl.BlockSpec((K, tn), lambda i, j: (0, j))],
        out_specs=pl.BlockSpec((tm, tn), lambda i, j: (i, j)),
        out_shape=jax.ShapeDtypeStruct((M, N), F32),
        scratch_shapes=[pltpu.VMEM((tm, K), BF16)],
        compiler_params=_params("parallel", "arbitrary"),
        name="norm_matmul",
    )(x, g, w)


def _merge_kernel(h_ref, ga_ref, gb_ref, ya_ref, yb_ref, wa_ref, wb_ref, o_ref):
    ma = (jax.nn.sigmoid(ga_ref[...]) * ya_ref[...]).astype(BF16)
    mb = (jax.nn.sigmoid(gb_ref[...]) * yb_ref[...]).astype(BF16)
    o_ref[...] = (h_ref[...] + jnp.dot(ma, wa_ref[...], preferred_element_type=F32)
                  + jnp.dot(mb, wb_ref[...], preferred_element_type=F32))


def merge_project(h, z, ya, yb, wa, wb, tm):
    M, D = h.shape
    row = pl.BlockSpec((tm, D), lambda i: (i, 0))
    wspec = pl.BlockSpec((D, D), lambda i: (0, 0), pipeline_mode=pl.Buffered(1))
    return pl.pallas_call(
        _merge_kernel,
        grid=(M // tm,),
        in_specs=[row,
                  pl.BlockSpec((tm, D), lambda i: (i, 6)),
                  pl.BlockSpec((tm, D), lambda i: (i, 7)),
                  row, row, wspec, wspec],
        out_specs=row,
        out_shape=jax.ShapeDtypeStruct((M, D), F32),
        compiler_params=_params("parallel"),
        name="merge_project",
    )(h, z, z, ya, yb, wa, wb)


def _ffn_kernel(h_ref, g_ref, wg_ref, wu_ref, wd_ref, o_ref, u_ref, acc_ref):
    f = pl.program_id(1)

    @pl.when(f == 0)
    def _():
        u_ref[...] = _rms(h_ref[...], g_ref[...]).astype(BF16)
        acc_ref[...] = jnp.zeros_like(acc_ref)

    u = u_ref[...]
    gate = jnp.dot(u, wg_ref[...], preferred_element_type=F32)
    up = jnp.dot(u, wu_ref[...], preferred_element_type=F32)
    act = (gate * jax.nn.sigmoid(gate) * up).astype(BF16)
    acc_ref[...] += jnp.dot(act, wd_ref[...], preferred_element_type=F32)

    @pl.when(f == pl.num_programs(1) - 1)
    def _():
        o_ref[...] = h_ref[...] + acc_ref[...]


def ffn(h, g, wg, wu, wd, tm, tf):
    M, D = h.shape
    F = wg.shape[1]
    return pl.pallas_call(
        _ffn_kernel,
        grid=(M // tm, F // tf),
        in_specs=[pl.BlockSpec((tm, D), lambda i, j: (i, 0)),
                  pl.BlockSpec((1, D), lambda i, j: (0, 0)),
                  pl.BlockSpec((D, tf), lambda i, j: (0, j)),
                  pl.BlockSpec((D, tf), lambda i, j: (0, j)),
                  pl.BlockSpec((tf, D), lambda i, j: (j, 0))],
        out_specs=pl.BlockSpec((tm, D), lambda i, j: (i, 0)),
        out_shape=jax.ShapeDtypeStruct((M, D), F32),
        scratch_shapes=[pltpu.VMEM((tm, D), BF16), pltpu.VMEM((tm, D), F32)],
        compiler_params=_params("parallel", "arbitrary"),
        name="ffn",
    )(h, g, wg, wu, wd)


def _ple_kernel(h_ref, g_ref, p_ref, wpg_ref, wpi_ref, gf_ref, o_ref, *, final):
    h = h_ref[...]
    u = _rms(h, g_ref[...]).astype(BF16)
    gate = jnp.dot(u, wpg_ref[...], preferred_element_type=F32)
    emb = jnp.dot(p_ref[...].astype(BF16), wpi_ref[...], preferred_element_type=F32)
    out = h + jax.nn.sigmoid(gate) * emb
    if final:
        out = _rms(out, gf_ref[...])
    o_ref[...] = out


def ple(h, g, p, wpg, wpi, g_final, tm, final):
    M, D = h.shape
    P = p.shape[1]
    return pl.pallas_call(
        functools.partial(_ple_kernel, final=final),
        grid=(M // tm,),
        in_specs=[pl.BlockSpec((tm, D), lambda i: (i, 0)),
                  pl.BlockSpec((1, D), lambda i: (0, 0)),
                  pl.BlockSpec((tm, P), lambda i: (i, 0)),
                  pl.BlockSpec((D, D), lambda i: (0, 0), pipeline_mode=pl.Buffered(1)),
                  pl.BlockSpec((P, D), lambda i: (0, 0), pipeline_mode=pl.Buffered(1)),
                  pl.BlockSpec((1, D), lambda i: (0, 0))],
        out_specs=pl.BlockSpec((tm, D), lambda i: (i, 0)),
        out_shape=jax.ShapeDtypeStruct((M, D), F32),
        compiler_params=_params("parallel"),
        name="ple",
    )(h, g, p, wpg, wpi, g_final)


def _mlstm_kernel(qk_ref, v_ref, o_ref, gt_ref, conv0_ref, cw_ref, cb_ref, gbias_ref, mg_ref,
                  c0_ref, n0_ref, m0_ref,
                  y_ref, convn_ref, c1_ref, n1_ref, m1_ref,
                  xp, C, n, m, *, gate_off, L):
    c = pl.program_id(1)
    rows, D = qk_ref.shape
    H = M_HEADS
    DV = D // H
    DK = DV // 2
    QK = H * DK
    halo = CONV_W - 1
    top = SUBLANES - halo

    @pl.when(c == 0)
    def _():
        xp[top:SUBLANES, :] = conv0_ref[...]
        C[...] = c0_ref[...]
        n[...] = n0_ref[...]
        m[...] = m0_ref[...]

    row = lax.broadcasted_iota(jnp.int32, (L, L), 0)
    col = lax.broadcasted_iota(jnp.int32, (L, L), 1)
    tril = row >= col
    eye = row == col

    def conv_silu(c0):
        acc = cb_ref[:, c0:c0 + DK] + xp[top:top + L, c0:c0 + DK] * cw_ref[0:1, c0:c0 + DK]
        for j in range(1, CONV_W):
            acc = acc + xp[top + j:top + j + L, c0:c0 + DK] * cw_ref[j:j + 1, c0:c0 + DK]
        return acc * jax.nn.sigmoid(acc)

    def chunk(i, carry):
        rs = pl.ds(pl.multiple_of(i * L, L), L)
        xp[SUBLANES:SUBLANES + L, :] = qk_ref[rs, :]
        gt = gt_ref[rs, :] + gbias_ref[...]
        for h in range(H):
            head(h, rs, gt)
        xp[top:SUBLANES, :] = xp[top + L:SUBLANES + L, :]
        return carry

    def head(h, rs, gt):
        q = conv_silu(h * DK)
        k = conv_silu(QK + h * DK) * (DK ** -0.5)
        vb = v_ref[rs, h * DV:(h + 1) * DV].astype(BF16)
        li_c = gt[:, gate_off + h:gate_off + h + 1]
        lf_c = _log_sigmoid(gt[:, gate_off + H + h:gate_off + H + h + 1])
        li_r = jnp.sum(jnp.where(eye, li_c, 0.0), axis=0, keepdims=True)
        b_row = jnp.sum(jnp.where(row <= col, lf_c, 0.0), axis=0, keepdims=True)
        b_col = jnp.sum(jnp.where(eye, b_row, 0.0), axis=1, keepdims=True)
        m_old = m[h]
        a_col = b_col + m_old
        dmat = jnp.where(tril, b_col - b_row + li_r, -jnp.inf)
        mt = jnp.maximum(a_col, jnp.max(dmat, axis=1, keepdims=True))
        ea = jnp.exp(a_col - mt)
        p = jnp.exp(dmat - mt)

        qb = q.astype(BF16)
        Ch = C[h]
        nh = n[h]
        s = lax.dot_general(qb, k.astype(BF16), (((1,), (1,)), ((), ())),
                            preferred_element_type=F32) * p
        num = (ea * jnp.dot(qb, Ch.astype(BF16), preferred_element_type=F32)
               + jnp.dot(s.astype(BF16), vb, preferred_element_type=F32))
        den = ea * jnp.sum(q * nh, axis=1, keepdims=True) + jnp.sum(s, axis=1, keepdims=True)
        hh = num / jnp.maximum(jnp.abs(den), jnp.exp(-mt))
        hh = hh * lax.rsqrt(jnp.mean(hh * hh, axis=1, keepdims=True) + EPS)
        y_ref[rs, h * DV:(h + 1) * DV] = (jax.nn.sigmoid(o_ref[rs, h * DV:(h + 1) * DV])
                                          * (hh * mg_ref[:, h * DV:(h + 1) * DV]))

        m_new = mt[L - 1:L, :]
        dec = jnp.exp(a_col[L - 1:L, :] - m_new)
        wk = jnp.exp(b_col[L - 1:L, :] - b_col + li_c - m_new)
        kw = k * wk
        C[h] = dec * Ch + lax.dot_general(kw.astype(BF16), vb, (((0,), (0,)), ((), ())),
                                          preferred_element_type=F32)
        n[h] = dec * nh + jnp.sum(kw, axis=0, keepdims=True)
        m[h] = m_new

    lax.fori_loop(0, rows // L, chunk, 0)

    @pl.when(c == pl.num_programs(1) - 1)
    def _():
        convn_ref[...] = xp[top:SUBLANES, :]
        c1_ref[...] = C[...]
        n1_ref[...] = n[...]
        m1_ref[...] = m[...]


def mlstm_branch(z, conv0, C0, n0, m0, conv_w, conv_b, gate_bias, m_norm_g, L, gate_off):
    B, T, _ = z.shape
    H = M_HEADS
    _, _, DK, DV = C0.shape
    D = H * DV
    halo = CONV_W - 1
    rows = L * _pick_tile(T // L, (MLSTM_CHUNKS_PER_STEP, 2, 1))

    def seq(j):
        return pl.BlockSpec((None, rows, D), lambda b, c: (b, c, j))

    def per_batch(*shape):
        return pl.BlockSpec((None,) + shape, lambda b, c: (b,) + (0,) * len(shape))

    def const(*shape):
        return pl.BlockSpec(shape, lambda b, c: (0,) * len(shape))

    gate_blk = (8 * D + GATE_BLOCK) // GATE_BLOCK
    y, conv_new, C1, n1, m1 = pl.pallas_call(
        functools.partial(_mlstm_kernel, gate_off=gate_off, L=L),
        grid=(B, T // rows),
        in_specs=[seq(0), seq(1), seq(2),
                  pl.BlockSpec((None, rows, GATE_BLOCK), lambda b, c: (b, c, gate_blk)),
                  per_batch(halo, D), const(CONV_W, D), const(1, D), const(1, GATE_BLOCK),
                  const(1, D),
                  per_batch(H, DK, DV), per_batch(H, 1, DK), per_batch(H, 1, 1)],
        out_specs=[seq(0), per_batch(halo, D),
                   per_batch(H, DK, DV), per_batch(H, 1, DK), per_batch(H, 1, 1)],
        out_shape=[jax.ShapeDtypeStruct((B, T, D), F32),
                   jax.ShapeDtypeStruct((B, halo, D), F32),
                   jax.ShapeDtypeStruct((B, H, DK, DV), F32),
                   jax.ShapeDtypeStruct((B, H, 1, DK), F32),
                   jax.ShapeDtypeStruct((B, H, 1, 1), F32)],
        scratch_shapes=[pltpu.VMEM((L + SUBLANES, D), F32), pltpu.VMEM((H, DK, DV), F32),
                        pltpu.VMEM((H, 1, DK), F32), pltpu.VMEM((H, 1, 1), F32)],
        compiler_params=_params("parallel", "arbitrary"),
        name="mlstm",
    )(z, z, z, z, conv0, conv_w, conv_b, gate_bias, m_norm_g,
      C0, n0[:, :, None, :], m0[:, :, None, None])
    return y, conv_new, C1, n1[:, :, 0, :], m1[:, :, 0, 0]


RW_BATCH = 4
DECAY_SCALE = 0.6065306597126334


def _block_transpose(a, rh):
    blk = lax.broadcasted_iota(jnp.int32, a[0].shape, 1) // rh
    out = []
    for g in range(4):
        acc = None
        for b in range(4):
            src = a[b] if b == g else pltpu.roll(a[b], ((b - g) % 4) * rh, axis=1)
            acc = src if acc is None else jnp.where(blk == b, src, acc)
        out.append(acc)
    return tuple(out)


def _rwkv_kernel(zr_ref, zg_ref, zs_ref, shp_r_ref, sh_g_ref, sh_s_ref,
                 mu_r_ref, mu_g_ref, mu_s_ref, ww2_ref, wa2_ref, wg2_ref,
                 w0_ref, a0_ref, kkw_ref, ka_ref, rk_ref, gnw_ref, gnb_ref, s0_ref,
                 y_ref, shn_r_ref, shn_g_ref, shn_s_ref, s1_ref,
                 S, car_r, car_g, car_s, PL, LW, LA, LG, R, W, K, KK, BT, VP, OP):
    c = pl.program_id(1)
    NB, Tc, D3 = zr_ref.shape
    D = D3 // 3
    RH = D // R_HEAD
    N = R_HEAD
    NT = D // LANES
    VH = N // SUBLANES

    @pl.when(c == 0)
    def _():
        S[...] = s0_ref[...]
        car_r[...] = shp_r_ref[...]
        car_g[...] = sh_g_ref[...]
        car_s[...] = sh_s_ref[...]

    first_row = lax.broadcasted_iota(jnp.int32, (Tc, LANES), 0) == 0

    def shifted(x, carry_row, mu):
        prev = jnp.where(first_row, carry_row, pltpu.roll(x, 1, axis=0))
        return x + mu * (prev - x)

    def tile(j):
        return slice(j * LANES, (j + 1) * LANES)

    def value_rows(n):
        return (n // SUBLANES, pl.ds(n % SUBLANES, Tc, stride=SUBLANES), slice(None))

    for bi in range(NB):
        xs = [shifted(zs_ref[bi, :, tile(j)], car_s[bi, :, tile(j)], mu_s_ref[:, tile(j)])
              for j in range(GATE_BLOCK // LANES)]
        xg = [shifted(zg_ref[bi, :, tile(j)], car_g[bi, :, tile(j)], mu_g_ref[:, tile(j)])
              for j in range(GATE_BLOCK // LANES)]
        s_small = jnp.concatenate(xs, axis=1)
        s_gate = jnp.concatenate(xg, axis=1)
        lw = jnp.dot(jnp.tanh(s_small).astype(BF16), ww2_ref[...], preferred_element_type=F32)
        la = jnp.dot(s_small.astype(BF16), wa2_ref[...], preferred_element_type=F32)
        lg = jnp.dot(jax.nn.sigmoid(s_gate).astype(BF16), wg2_ref[...],
                     preferred_element_type=F32)
        for j in range(NT):
            LW[bi * NT + j] = lw[:, tile(j)]
            LA[bi * NT + j] = la[:, tile(j)]
            LG[bi * NT + j] = lg[:, tile(j)]
        car_g[bi] = zg_ref[bi, Tc - 1:Tc, :]
        car_s[bi] = zs_ref[bi, Tc - 1:Tc, :]

    for j in range(NT):
        for blk, dst in enumerate((R, K, None)):
            packed = _block_transpose([zr_ref[b, :, tile(blk * NT + j)] for b in range(NB)], RH)
            for g in range(NB):
                if dst is None:
                    VP[value_rows(NB * j + g)] = packed[g]
                else:
                    dst[NB * j + g] = packed[g]
        for src, dst in ((LW, W), (LA, BT)):
            packed = _block_transpose([src[b * NT + j] for b in range(NB)], RH)
            for g in range(NB):
                dst[NB * j + g] = packed[g]

    t_idx = lax.broadcasted_iota(jnp.int32, (Tc, LANES), 0)
    ss = jnp.zeros((Tc, LANES), F32)
    for n in range(N):
        raw = (R[n], K[n], VP[value_rows(n)])
        r, k, v = (shifted(raw[blk], car_r[blk * N + n:blk * N + n + 1, :],
                           mu_r_ref[blk * N + n:blk * N + n + 1, :]) for blk in range(3))
        for blk in range(3):
            car_r[blk * N + n:blk * N + n + 1, :] = raw[blk][Tc - 1:Tc, :]
        a = jax.nn.sigmoid(a0_ref[n:n + 1, :] + BT[n])
        kk = k * kkw_ref[n:n + 1, :]
        ss = ss + kk * kk
        log_w = -DECAY_SCALE * jax.nn.sigmoid(w0_ref[n:n + 1, :] + W[n])
        cum = log_w
        shift = 1
        while shift < Tc:
            cum = cum + jnp.where(t_idx >= shift, pltpu.roll(cum, shift, axis=0), 0.0)
            shift *= 2
        p_inv = jnp.exp(-cum)
        R[n] = r * jnp.exp(cum)
        VP[value_rows(n)] = v
        KK[n] = kk * jnp.exp(cum - log_w)
        K[n] = k * (1.0 + (a - 1.0) * ka_ref[n:n + 1, :]) * p_inv
        BT[n] = kk * a * p_inv
        PL[n:n + 1, :] = jnp.exp(cum[Tc - 1:Tc, :])
    inv = 1.0 / jnp.maximum(jnp.sqrt(ss), 1e-12)
    for n in range(N):
        KK[n] = KK[n] * inv
        BT[n] = BT[n] * inv

    def step(t, carry):
        row = pl.ds(t, 1)
        blk = pl.ds(pl.multiple_of(t * SUBLANES, SUBLANES), SUBLANES)
        acc = [[None, None] for _ in range(VH)]
        for key in range(N):
            kk_row = KK[key, row, :]
            for vh in range(VH):
                term = S[key, vh] * kk_row
                prev = acc[vh][key % 2]
                acc[vh][key % 2] = term if prev is None else prev + term
        sa = [-(acc[vh][0] + acc[vh][1]) for vh in range(VH)]
        vv = [VP[vh, blk, :] for vh in range(VH)]
        out = [[None, None] for _ in range(VH)]
        for key in range(N):
            b_row = BT[key, row, :]
            k_row = K[key, row, :]
            r_row = R[key, row, :]
            for vh in range(VH):
                s_new = S[key, vh] + sa[vh] * b_row + vv[vh] * k_row
                S[key, vh] = s_new
                term = s_new * r_row
                prev = out[vh][key % 2]
                out[vh][key % 2] = term if prev is None else prev + term
        for vh in range(VH):
            OP[vh, blk, :] = out[vh][0] + out[vh][1]
        return carry

    lax.fori_loop(0, Tc, step, 0, unroll=8)
    for key in range(N):
        for vh in range(VH):
            S[key, vh] = S[key, vh] * PL[key:key + 1, :]

    inv_n = 1.0 / N
    tot = jnp.zeros((Tc, LANES), F32)
    bon = jnp.zeros((Tc, LANES), F32)
    for n in range(N):
        tot = tot + OP[value_rows(n)]
        bon = bon + R[n] * K[n] * rk_ref[n:n + 1, :]
    mean = tot * inv_n
    var = jnp.zeros((Tc, LANES), F32)
    for n in range(N):
        d = OP[value_rows(n)] - mean
        var = var + d * d
    rstd = lax.rsqrt(var * inv_n + GN_EPS)
    for j in range(NT):
        ys = []
        for g in range(NB):
            n = NB * j + g
            o = (OP[value_rows(n)] - mean) * rstd * gnw_ref[n:n + 1, :] + gnb_ref[n:n + 1, :]
            ys.append(o + bon * VP[value_rows(n)])
        for b, y in enumerate(_block_transpose(ys, RH)):
            y_ref[b, :, tile(j)] = y * LG[b * NT + j]

    @pl.when(c == pl.num_programs(1) - 1)
    def _():
        s1_ref[...] = S[...]
        for b in range(NB):
            shn_r_ref[b] = zr_ref[b, Tc - 1:Tc, :]
        shn_g_ref[...] = car_g[...]
        shn_s_ref[...] = car_s[...]


def rwkv_branch(z, shp_r, sh_g, sh_s, S0, P, steps):
    B, T, _ = z.shape
    N = R_HEAD
    NB = RW_BATCH
    D = LANES // NB * N
    NT = D // LANES
    VH = N // SUBLANES
    assert B % NB == 0 and T % steps == 0 and steps % SUBLANES == 0
    g_blk = 8 * D // GATE_BLOCK

    def seq(width, j):
        return pl.BlockSpec((NB, steps, width), lambda g, c: (g, c, j))

    def per_batch(*shape):
        return pl.BlockSpec((NB,) + shape, lambda g, c: (g,) + (0,) * len(shape))

    def per_group(*shape):
        return pl.BlockSpec((None,) + shape, lambda g, c: (g,) + (0,) * len(shape))

    def const(*shape):
        return pl.BlockSpec(shape, lambda g, c: (0,) * len(shape))

    packed = pltpu.VMEM((N, steps, LANES), F32)
    natural = pltpu.VMEM((NB * NT, steps, LANES), F32)
    by_value = pltpu.VMEM((VH, steps * SUBLANES, LANES), F32)
    y, shn_r, shn_g, shn_s, S1 = pl.pallas_call(
        _rwkv_kernel,
        grid=(B // NB, T // steps),
        in_specs=[seq(3 * D, 1), seq(GATE_BLOCK, g_blk), seq(GATE_BLOCK, g_blk + 1),
                  per_group(3 * N, LANES), per_batch(1, GATE_BLOCK), per_batch(1, GATE_BLOCK),
                  const(3 * N, LANES), const(1, GATE_BLOCK), const(1, GATE_BLOCK),
                  const(GATE_BLOCK, D), const(GATE_BLOCK, D), const(GATE_BLOCK, D)]
                 + [const(N, LANES)] * 7
                 + [per_group(N, VH, SUBLANES, LANES)],
        out_specs=[seq(D, 0), per_batch(1, 3 * D), per_batch(1, GATE_BLOCK),
                   per_batch(1, GATE_BLOCK), per_group(N, VH, SUBLANES, LANES)],
        out_shape=[jax.ShapeDtypeStruct((B, T, D), F32),
                   jax.ShapeDtypeStruct((B, 1, 3 * D), F32),
                   jax.ShapeDtypeStruct((B, 1, GATE_BLOCK), F32),
                   jax.ShapeDtypeStruct((B, 1, GATE_BLOCK), F32),
                   jax.ShapeDtypeStruct((B // NB, N, VH, SUBLANES, LANES), F32)],
        scratch_shapes=[pltpu.VMEM((N, VH, SUBLANES, LANES), F32),
                        pltpu.VMEM((3 * N, LANES), F32), pltpu.VMEM((NB, 1, GATE_BLOCK), F32),
                        pltpu.VMEM((NB, 1, GATE_BLOCK), F32), pltpu.VMEM((N, LANES), F32)]
                       + [natural] * 3 + [packed] * 5 + [by_value] * 2,
        compiler_params=_params("parallel", "arbitrary"),
        name="rwkv7",
    )(z, z, z, shp_r, sh_g, sh_s, P["mu_r"], P["mu_g"], P["mu_s"],
      P["w_w2"], P["w_a2"], P["w_g2"], P["w0"], P["a0"], P["k_k"], P["k_a"], P["r_k"],
      P["gn_w"], P["gn_b"], S0)
    return y, shn_r, shn_g, shn_s, S1


def _nh(x):
    lead = x.shape[:-1]
    return x.reshape(lead + (-1, R_HEAD)).swapaxes(-1, -2).reshape(x.shape)


def _hn(x):
    lead = x.shape[:-1]
    return x.reshape(lead + (R_HEAD, -1)).swapaxes(-1, -2).reshape(x.shape)


def _lane_vec(x):
    return jnp.tile(x.reshape(-1, R_HEAD).T, (1, RW_BATCH))


def _pack_rows(x):
    B = x.shape[0]
    x = x.reshape(B // RW_BATCH, RW_BATCH, -1, R_HEAD)
    return x.transpose(0, 3, 1, 2).reshape(B // RW_BATCH, R_HEAD, LANES)


def _pack_state(S):
    B, RH = S.shape[:2]
    S = S.reshape(B // RW_BATCH, RW_BATCH, RH, R_HEAD // SUBLANES, SUBLANES, R_HEAD)
    return S.transpose(0, 5, 3, 4, 1, 2).reshape(B // RW_BATCH, R_HEAD, R_HEAD // SUBLANES,
                                                 SUBLANES, LANES)


def _unpack_state(S):
    G = S.shape[0]
    RH = LANES // RW_BATCH
    S = S.reshape(G, R_HEAD, R_HEAD // SUBLANES, SUBLANES, RW_BATCH, RH)
    return S.transpose(0, 4, 5, 2, 3, 1).reshape(G * RW_BATCH, RH, R_HEAD, R_HEAD)


def _pick_tile(n, candidates):
    for c in candidates:
        if n % c == 0:
            return c
    return n


def _repack_w_in(w, D, lora_w, lora_a, lora_g):
    H = M_HEADS
    o_end = 3 * D
    f_end = o_end + 2 * H
    zr = f_end
    xw = zr + 3 * D
    xa = xw + lora_w
    xg = xa + lora_a
    ga = xg + lora_g
    small = lora_w + lora_a + 2 * H
    assert lora_g == GATE_BLOCK and small <= GATE_BLOCK
    pad = jnp.zeros((w.shape[0], GATE_BLOCK - small), w.dtype)
    rkv = [_nh(w[:, zr + i * D:zr + (i + 1) * D]) for i in range(3)]
    return jnp.concatenate(
        [w[:, :o_end]] + rkv + [w[:, ga:ga + D], _nh(w[:, ga + D:ga + 2 * D]), w[:, xg:ga],
                                w[:, xw:xg], w[:, o_end:f_end], pad], axis=1)


def _layer(h, p_l, conv0, C0, n0, m0, shift0, S0, chunk, W, final, g_final):
    B, T, D = h.shape
    M = B * T
    tm = min(512, M)
    lora_w, lora_a, _ = W["lora"]
    gate_off = lora_w + lora_a

    h2 = h.reshape(M, D)
    z2 = norm_matmul(h2, W["g_mix"], W["w_in"], min(1024, M), W["w_in_tn"])
    z = z2.reshape(B, T, -1)

    y_a, conv_new, C1, n1, m1 = mlstm_branch(z, conv0, C0, n0, m0, W["conv_w"], W["conv_b"],
                                             W["gate_bias"], W["m_norm_g"], chunk, gate_off)

    shp_r = jnp.concatenate([_pack_rows(shift0[:, 0, i * D:(i + 1) * D]) for i in range(3)], axis=1)
    sh_g = shift0[..., 3 * D + gate_off:]
    sh_s = jnp.pad(shift0[..., 3 * D:3 * D + gate_off], ((0, 0), (0, 0), (0, GATE_BLOCK - gate_off)))
    y_b, shn_r, shn_g, shn_s, S1 = rwkv_branch(z, shp_r, sh_g, sh_s, _pack_state(S0), W["rwkv"],
                                               min(32, T))
    shift_new = jnp.concatenate([_hn(shn_r[..., i * D:(i + 1) * D]) for i in range(3)]
                                + [shn_s[..., :gate_off], shn_g], axis=-1)
    S1 = _unpack_state(S1)

    h2 = merge_project(h2, z2, y_a.reshape(M, D), y_b.reshape(M, D), W["w_out"], W["w_out_nh"],
                       min(256, M))
    h2 = ffn(h2, W["g_ffn"], W["w_gate"], W["w_up"], W["w_down"], tm, W["ffn_tf"])
    h2 = ple(h2, W["g_ple"], p_l.reshape(M, -1), W["w_ple_gate"], W["w_ple_in"], g_final, tm, final)
    return h2.reshape(B, T, D), (conv_new, C1, n1, m1, shift_new, S1)


def kernel(x_prompt, x_sample, p_prompt, p_sample, state_mlstm_C, state_mlstm_n, state_mlstm_m, state_mlstm_conv, state_rwkv_wkv, state_rwkv_shift, g_mix, w_in, conv_w, conv_b, b_i, b_f, m_norm_g, rwkv_mu, w0, w_w2, a0, w_a2, w_g2, k_k, k_a, r_k, gn_w, gn_b, w_out, g_ffn, w_gate, w_up, w_down, g_ple, w_ple_gate, w_ple_in, g_final):
    depth = w_in.shape[0]
    D = x_prompt.shape[-1]
    H = M_HEADS
    DV = D // H
    DK = DV // 2
    QK = H * DK
    RH = D // R_HEAD
    r_shift = state_rwkv_shift.shape[-1]
    lora = (w_w2.shape[1], w_a2.shape[1], w_g2.shape[1])
    n_pack = 8 * D + 2 * GATE_BLOCK
    w_in_tn = _pick_tile(n_pack, (1536, 768, 512, 256, 128))

    layers = []
    for l in range(depth):
        gate_bias = jnp.zeros((1, GATE_BLOCK), F32)
        gate_bias = gate_bias.at[0, lora[0] + lora[1]:lora[0] + lora[1] + 2 * H].set(
            jnp.concatenate([b_i[l], b_f[l]]))
        mu = rwkv_mu[l]
        lw, la = lora[0], lora[1]
        w_w2p = jnp.zeros((GATE_BLOCK, D), F32).at[:lw].set(_nh(w_w2[l]))
        w_a2p = jnp.zeros((GATE_BLOCK, D), F32).at[lw:lw + la].set(_nh(w_a2[l]))
        rwkv = dict(
            mu_r=jnp.concatenate([_lane_vec(mu[i * D:(i + 1) * D]) for i in range(3)]),
            mu_g=mu[3 * D + lw + la:][None],
            mu_s=jnp.pad(mu[3 * D:3 * D + lw + la], (0, GATE_BLOCK - lw - la))[None],
            w_w2=w_w2p.astype(BF16), w_a2=w_a2p.astype(BF16), w_g2=_nh(w_g2[l]).astype(BF16),
            w0=_lane_vec(w0[l]), a0=_lane_vec(a0[l]), k_k=_lane_vec(k_k[l]),
            k_a=_lane_vec(k_a[l]), r_k=_lane_vec(r_k[l].reshape(-1)),
            gn_w=_lane_vec(gn_w[l]), gn_b=_lane_vec(gn_b[l]))
        layers.append(dict(
            g_mix=g_mix[l][None], w_in=_repack_w_in(w_in[l], D, *lora).astype(BF16),
            w_in_tn=w_in_tn, lora=lora,
            conv_w=conv_w[l], conv_b=conv_b[l][None], gate_bias=gate_bias,
            m_norm_g=m_norm_g[l][None], rwkv=rwkv,
            w_out=w_out[l].astype(BF16),
            w_out_nh=_nh(w_out[l].T).T.astype(BF16), g_ffn=g_ffn[l][None],
            w_gate=w_gate[l].astype(BF16), w_up=w_up[l].astype(BF16),
            w_down=w_down[l].astype(BF16), ffn_tf=_pick_tile(w_gate.shape[2], (512, 256, 128)),
            g_ple=g_ple[l][None], w_ple_gate=w_ple_gate[l].astype(BF16),
            w_ple_in=w_ple_in[l].astype(BF16)))
    gf = g_final[None]

    def run(x, p, conv0, C0, n0, m0, shift0, S0, chunk):
        h = x
        outs = ([], [], [], [], [], [])
        for l in range(depth):
            h, st = _layer(h, p[l], conv0[l], C0[l], n0[l], m0[l], shift0[l], S0[l], chunk,
                           layers[l], l == depth - 1, gf)
            for lst, s in zip(outs, st):
                lst.append(s)
        return h, [jnp.stack(o, axis=0) for o in outs]

    Bp = x_prompt.shape[0]
    y_prompt, (conv_p, C_p, n_p, m_p, shift_p, wkv_p) = run(
        x_prompt, p_prompt,
        jnp.zeros((depth, Bp, CONV_W - 1, 2 * QK), F32),
        jnp.zeros((depth, Bp, H, DK, DV), F32),
        jnp.zeros((depth, Bp, H, DK), F32),
        jnp.zeros((depth, Bp, H), F32),
        jnp.zeros((depth, Bp, 1, r_shift), F32),
        jnp.zeros((depth, Bp, RH, R_HEAD, R_HEAD), F32),
        CHUNK)
    y_sample, (conv_s, C_s, n_s, m_s, shift_s, wkv_s) = run(
        x_sample, p_sample, state_mlstm_conv, state_mlstm_C, state_mlstm_n, state_mlstm_m,
        state_rwkv_shift, state_rwkv_wkv, x_sample.shape[1])
    return (y_prompt, y_sample, C_p, n_p, m_p, conv_p, wkv_p, shift_p,
            C_s, n_s, m_s, conv_s, wkv_s, shift_s)
```

```python
import functools

import jax
import jax.numpy as jnp
from jax import lax
from jax.experimental import pallas as pl
from jax.experimental.pallas import tpu as pltpu

F32 = jnp.float32
BF16 = jnp.bfloat16

EPS = 1e-6
GN_EPS = 64e-5
M_HEADS = 4
CONV_W = 4
R_HEAD = 64
CHUNK = 64
MLSTM_CHUNKS_PER_STEP = 4
LANES = 128
SUBLANES = 8
VMEM_LIMIT = 48 * 1024 * 1024

GATE_BLOCK = 256


def _params(*sem):
    return pltpu.CompilerParams(dimension_semantics=sem, vmem_limit_bytes=VMEM_LIMIT)


def _rms(x, g):
    return x * lax.rsqrt(jnp.mean(x * x, axis=-1, keepdims=True) + EPS) * g


def _log_sigmoid(x):
    return jnp.minimum(x, 0.0) - jnp.log1p(jnp.exp(-jnp.abs(x)))


def _norm_mm_kernel(x_ref, g_ref, w_ref, o_ref, u_ref):
    @pl.when(pl.program_id(1) == 0)
    def _():
        u_ref[...] = _rms(x_ref[...], g_ref[...]).astype(BF16)

    o_ref[...] = jnp.dot(u_ref[...], w_ref[...], preferred_element_type=F32)


def norm_matmul(x, g, w, tm, tn):
    M, K = x.shape
    N = w.shape[1]
    return pl.pallas_call(
        _norm_mm_kernel,
        grid=(M // tm, N // tn),
        in_specs=[pl.BlockSpec((tm, K), lambda i, j: (i, 0)),
                  pl.BlockSpec((1, K), lambda i, j: (0, 0)),
                  pl.BlockSpec((K, tn), lambda i, j: (0, j))],
        out_specs=pl.BlockSpec((tm, tn), lambda i, j: (i, j)),
        out_shape=jax.ShapeDtypeStruct((M, N), F32),
        scratch_shapes=[pltpu.VMEM((tm, K), BF16)],
        compiler_params=_params("parallel", "arbitrary"),
        name="norm_matmul",
    )(x, g, w)


def _merge_kernel(h_ref, ga_ref, gb_ref, ya_ref, yb_ref, wa_ref, wb_ref, o_ref):
    ma = (jax.nn.sigmoid(ga_ref[...]) * ya_ref[...]).astype(BF16)
    mb = (jax.nn.sigmoid(gb_ref[...]) * yb_ref[...]).astype(BF16)
    o_ref[...] = (h_ref[...] + jnp.dot(ma, wa_ref[...], preferred_element_type=F32)
                  + jnp.dot(mb, wb_ref[...], preferred_element_type=F32))


def merge_project(h, z, ya, yb, wa, wb, tm):
    M, D = h.shape
    row = pl.BlockSpec((tm, D), lambda i: (i, 0))
    wspec = pl.BlockSpec((D, D), lambda i: (0, 0), pipeline_mode=pl.Buffered(1))
    return pl.pallas_call(
        _merge_kernel,
        grid=(M // tm,),
        in_specs=[row,
                  pl.BlockSpec((tm, D), lambda i: (i, 6)),
                  pl.BlockSpec((tm, D), lambda i: (i, 7)),
                  row, row, wspec, wspec],
        out_specs=row,
        out_shape=jax.ShapeDtypeStruct((M, D), F32),
        compiler_params=_params("parallel"),
        name="merge_project",
    )(h, z, z, ya, yb, wa, wb)


def _ffn_kernel(h_ref, g_ref, wg_ref, wu_ref, wd_ref, o_ref, u_ref, acc_ref):
    f = pl.program_id(1)

    @pl.when(f == 0)
    def _():
        u_ref[...] = _rms(h_ref[...], g_ref[...]).astype(BF16)
        acc_ref[...] = jnp.zeros_like(acc_ref)

    u = u_ref[...]
    gate = jnp.dot(u, wg_ref[...], preferred_element_type=F32)
    up = jnp.dot(u, wu_ref[...], preferred_element_type=F32)
    act = (gate * jax.nn.sigmoid(gate) * up).astype(BF16)
    acc_ref[...] += jnp.dot(act, wd_ref[...], preferred_element_type=F32)

    @pl.when(f == pl.num_programs(1) - 1)
    def _():
        o_ref[...] = h_ref[...] + acc_ref[...]


def ffn(h, g, wg, wu, wd, tm, tf):
    M, D = h.shape
    F = wg.shape[1]
    return pl.pallas_call(
        _ffn_kernel,
        grid=(M // tm, F // tf),
        in_specs=[pl.BlockSpec((tm, D), lambda i, j: (i, 0)),
                  pl.BlockSpec((1, D), lambda i, j: (0, 0)),
                  pl.BlockSpec((D, tf), lambda i, j: (0, j)),
                  pl.BlockSpec((D, tf), lambda i, j: (0, j)),
                  pl.BlockSpec((tf, D), lambda i, j: (j, 0))],
        out_specs=pl.BlockSpec((tm, D), lambda i, j: (i, 0)),
        out_shape=jax.ShapeDtypeStruct((M, D), F32),
        scratch_shapes=[pltpu.VMEM((tm, D), BF16), pltpu.VMEM((tm, D), F32)],
        compiler_params=_params("parallel", "arbitrary"),
        name="ffn",
    )(h, g, wg, wu, wd)


def _ple_kernel(h_ref, g_ref, p_ref, wpg_ref, wpi_ref, gf_ref, o_ref, *, final):
    h = h_ref[...]
    u = _rms(h, g_ref[...]).astype(BF16)
    gate = jnp.dot(u, wpg_ref[...], preferred_element_type=F32)
    emb = jnp.dot(p_ref[...].astype(BF16), wpi_ref[...], preferred_element_type=F32)
    out = h + jax.nn.sigmoid(gate) * emb
    if final:
        out = _rms(out, gf_ref[...])
    o_ref[...] = out


def ple(h, g, p, wpg, wpi, g_final, tm, final):
    M, D = h.shape
    P = p.shape[1]
    return pl.pallas_call(
        functools.partial(_ple_kernel, final=final),
        grid=(M // tm,),
        in_specs=[pl.BlockSpec((tm, D), lambda i: (i, 0)),
                  pl.BlockSpec((1, D), lambda i: (0, 0)),
                  pl.BlockSpec((tm, P), lambda i: (i, 0)),
                  pl.BlockSpec((D, D), lambda i: (0, 0), pipeline_mode=pl.Buffered(1)),
                  pl.BlockSpec((P, D), lambda i: (0, 0), pipeline_mode=pl.Buffered(1)),
                  pl.BlockSpec((1, D), lambda i: (0, 0))],
        out_specs=pl.BlockSpec((tm, D), lambda i: (i, 0)),
        out_shape=jax.ShapeDtypeStruct((M, D), F32),
        compiler_params=_params("parallel"),
        name="ple",
    )(h, g, p, wpg, wpi, g_final)


def _mlstm_kernel(qk_ref, v_ref, o_ref, gt_ref, conv0_ref, cw_ref, cb_ref, gbias_ref, mg_ref,
                  c0_ref, n0_ref, m0_ref,
                  y_ref, convn_ref, c1_ref, n1_ref, m1_ref,
                  xp, C, n, m, *, gate_off, L):
    c = pl.program_id(1)
    rows, D = qk_ref.shape
    H = M_HEADS
    DV = D // H
    DK = DV // 2
    QK = H * DK
    halo = CONV_W - 1
    top = SUBLANES - halo

    @pl.when(c == 0)
    def _():
        xp[top:SUBLANES, :] = conv0_ref[...]
        C[...] = c0_ref[...]
        n[...] = n0_ref[...]
        m[...] = m0_ref[...]

    row = lax.broadcasted_iota(jnp.int32, (L, L), 0)
    col = lax.broadcasted_iota(jnp.int32, (L, L), 1)
    tril = row >= col
    eye = row == col

    def conv_silu(c0):
        acc = cb_ref[:, c0:c0 + DK] + xp[top:top + L, c0:c0 + DK] * cw_ref[0:1, c0:c0 + DK]
        for j in range(1, CONV_W):
            acc = acc + xp[top + j:top + j + L, c0:c0 + DK] * cw_ref[j:j + 1, c0:c0 + DK]
        return acc * jax.nn.sigmoid(acc)

    def chunk(i, carry):
        rs = pl.ds(pl.multiple_of(i * L, L), L)
        xp[SUBLANES:SUBLANES + L, :] = qk_ref[rs, :]
        gt = gt_ref[rs, :] + gbias_ref[...]
        for h in range(H):
            head(h, rs, gt)
        xp[top:SUBLANES, :] = xp[top + L:SUBLANES + L, :]
        return carry

    def head(h, rs, gt):
        q = conv_silu(h * DK)
        k = conv_silu(QK + h * DK) * (DK ** -0.5)
        vb = v_ref[rs, h * DV:(h + 1) * DV].astype(BF16)
        li_c = gt[:, gate_off + h:gate_off + h + 1]
        lf_c = _log_sigmoid(gt[:, gate_off + H + h:gate_off + H + h + 1])
        li_r = jnp.sum(jnp.where(eye, li_c, 0.0), axis=0, keepdims=True)
        b_row = jnp.sum(jnp.where(row <= col, lf_c, 0.0), axis=0, keepdims=True)
        b_col = jnp.sum(jnp.where(eye, b_row, 0.0), axis=1, keepdims=True)
        m_old = m[h]
        a_col = b_col + m_old
        dmat = jnp.where(tril, b_col - b_row + li_r, -jnp.inf)
        mt = jnp.maximum(a_col, jnp.max(dmat, axis=1, keepdims=True))
        ea = jnp.exp(a_col - mt)
        p = jnp.exp(dmat - mt)

        qb = q.astype(BF16)
        Ch = C[h]
        nh = n[h]
        s = lax.dot_general(qb, k.astype(BF16), (((1,), (1,)), ((), ())),
                            preferred_element_type=F32) * p
        num = (ea * jnp.dot(qb, Ch.astype(BF16), preferred_element_type=F32)
               + jnp.dot(s.astype(BF16), vb, preferred_element_type=F32))
        den = ea * jnp.sum(q * nh, axis=1, keepdims=True) + jnp.sum(s, axis=1, keepdims=True)
        hh = num / jnp.maximum(jnp.abs(den), jnp.exp(-mt))
        hh = hh * lax.rsqrt(jnp.mean(hh * hh, axis=1, keepdims=True) + EPS)
        y_ref[rs, h * DV:(h + 1) * DV] = (jax.nn.sigmoid(o_ref[rs, h * DV:(h + 1) * DV])
                                          * (hh * mg_ref[:, h * DV:(h + 1) * DV])).astype(BF16)

        m_new = mt[L - 1:L, :]
        dec = jnp.exp(a_col[L - 1:L, :] - m_new)
        wk = jnp.exp(b_col[L - 1:L, :] - b_col + li_c - m_new)
        kw = k * wk
        C[h] = dec * Ch + lax.dot_general(kw.astype(BF16), vb, (((0,), (0,)), ((), ())),
                                          preferred_element_type=F32)
        n[h] = dec * nh + jnp.sum(kw, axis=0, keepdims=True)
        m[h] = m_new

    lax.fori_loop(0, rows // L, chunk, 0)

    @pl.when(c == pl.num_programs(1) - 1)
    def _():
        convn_ref[...] = xp[top:SUBLANES, :]
        c1_ref[...] = C[...]
        n1_ref[...] = n[...]
        m1_ref[...] = m[...]


def mlstm_branch(z, conv0, C0, n0, m0, conv_w, conv_b, gate_bias, m_norm_g, L, gate_off):
    B, T, _ = z.shape
    H = M_HEADS
    _, _, DK, DV = C0.shape
    D = H * DV
    halo = CONV_W - 1
    rows = L * _pick_tile(T // L, (MLSTM_CHUNKS_PER_STEP, 2, 1))

    def seq(j):
        return pl.BlockSpec((None, rows, D), lambda b, c: (b, c, j))

    def per_batch(*shape):
        return pl.BlockSpec((None,) + shape, lambda b, c: (b,) + (0,) * len(shape))

    def const(*shape):
        return pl.BlockSpec(shape, lambda b, c: (0,) * len(shape))

    gate_blk = (8 * D + GATE_BLOCK) // GATE_BLOCK
    y, conv_new, C1, n1, m1 = pl.pallas_call(
        functools.partial(_mlstm_kernel, gate_off=gate_off, L=L),
        grid=(B, T // rows),
        in_specs=[seq(0), seq(1), seq(2),
                  pl.BlockSpec((None, rows, GATE_BLOCK), lambda b, c: (b, c, gate_blk)),
                  per_batch(halo, D), const(CONV_W, D), const(1, D), const(1, GATE_BLOCK),
                  const(1, D),
                  per_batch(H, DK, DV), per_batch(H, 1, DK), per_batch(H, 1, 1)],
        out_specs=[seq(0), per_batch(halo, D),
                   per_batch(H, DK, DV), per_batch(H, 1, DK), per_batch(H, 1, 1)],
        out_shape=[jax.ShapeDtypeStruct((B, T, D), BF16),
                   jax.ShapeDtypeStruct((B, halo, D), F32),
                   jax.ShapeDtypeStruct((B, H, DK, DV), F32),
                   jax.ShapeDtypeStruct((B, H, 1, DK), F32),
                   jax.ShapeDtypeStruct((B, H, 1, 1), F32)],
        scratch_shapes=[pltpu.VMEM((L + SUBLANES, D), F32), pltpu.VMEM((H, DK, DV), F32),
                        pltpu.VMEM((H, 1, DK), F32), pltpu.VMEM((H, 1, 1), F32)],
        compiler_params=_params("parallel", "arbitrary"),
        name="mlstm",
    )(z, z, z, z, conv0, conv_w, conv_b, gate_bias, m_norm_g,
      C0, n0[:, :, None, :], m0[:, :, None, None])
    return y, conv_new, C1, n1[:, :, 0, :], m1[:, :, 0, 0]


RW_BATCH = 4
DECAY_SCALE = 0.6065306597126334


def _block_transpose(a, rh):
    blk = lax.broadcasted_iota(jnp.int32, a[0].shape, 1) // rh
    out = []
    for g in range(4):
        acc = None
        for b in range(4):
            src = a[b] if b == g else pltpu.roll(a[b], ((b - g) % 4) * rh, axis=1)
            acc = src if acc is None else jnp.where(blk == b, src, acc)
        out.append(acc)
    return tuple(out)


def _rwkv_kernel(zr_ref, zg_ref, zs_ref, shp_r_ref, sh_g_ref, sh_s_ref,
                 mu_r_ref, mu_g_ref, mu_s_ref, ww2_ref, wa2_ref, wg2_ref,
                 w0_ref, a0_ref, kkw_ref, ka_ref, rk_ref, gnw_ref, gnb_ref, s0_ref,
                 y_ref, shn_r_ref, shn_g_ref, shn_s_ref, s1_ref,
                 S, car_r, car_g, car_s, PL, LW, LA, LG, R, W, K, KK, BT, VP, OP):
    c = pl.program_id(1)
    NB, Tc, D3 = zr_ref.shape
    D = D3 // 3
    RH = D // R_HEAD
    N = R_HEAD
    NT = D // LANES
    VH = N // SUBLANES

    @pl.when(c == 0)
    def _():
        S[...] = s0_ref[...]
        car_r[...] = shp_r_ref[...]
        car_g[...] = sh_g_ref[...]
        car_s[...] = sh_s_ref[...]

    first_row = lax.broadcasted_iota(jnp.int32, (Tc, LANES), 0) == 0

    def shifted(x, carry_row, mu):
        prev = jnp.where(first_row, carry_row, pltpu.roll(x, 1, axis=0))
        return x + mu * (prev - x)

    def tile(j):
        return slice(j * LANES, (j + 1) * LANES)

    def value_rows(n):
        return (n // SUBLANES, pl.ds(n % SUBLANES, Tc, stride=SUBLANES), slice(None))

    for bi in range(NB):
        xs = [shifted(zs_ref[bi, :, tile(j)], car_s[bi, :, tile(j)], mu_s_ref[:, tile(j)])
              for j in range(GATE_BLOCK // LANES)]
        xg = [shifted(zg_ref[bi, :, tile(j)], car_g[bi, :, tile(j)], mu_g_ref[:, tile(j)])
              for j in range(GATE_BLOCK // LANES)]
        s_small = jnp.concatenate(xs, axis=1)
        s_gate = jnp.concatenate(xg, axis=1)
        lw = jnp.dot(jnp.tanh(s_small).astype(BF16), ww2_ref[...], preferred_element_type=F32)
        la = jnp.dot(s_small.astype(BF16), wa2_ref[...], preferred_element_type=F32)
        lg = jnp.dot(jax.nn.sigmoid(s_gate).astype(BF16), wg2_ref[...],
                     preferred_element_type=F32)
        for j in range(NT):
            LW[bi * NT + j] = lw[:, tile(j)]
            LA[bi * NT + j] = la[:, tile(j)]
            LG[bi * NT + j] = lg[:, tile(j)]
        car_g[bi] = zg_ref[bi, Tc - 1:Tc, :]
        car_s[bi] = zs_ref[bi, Tc - 1:Tc, :]

    for j in range(NT):
        for blk, dst in enumerate((R, K, None)):
            packed = _block_transpose([zr_ref[b, :, tile(blk * NT + j)] for b in range(NB)], RH)
            for g in range(NB):
                if dst is None:
                    VP[value_rows(NB * j + g)] = packed[g]
                else:
                    dst[NB * j + g] = packed[g]
        for src, dst in ((LW, W), (LA, BT)):
            packed = _block_transpose([src[b * NT + j] for b in range(NB)], RH)
            for g in range(NB):
                dst[NB * j + g] = packed[g]

    t_idx = lax.broadcasted_iota(jnp.int32, (Tc, LANES), 0)
    ss = jnp.zeros((Tc, LANES), F32)
    for n in range(N):
        raw = (R[n], K[n], VP[value_rows(n)])
        r, k, v = (shifted(raw[blk], car_r[blk * N + n:blk * N + n + 1, :],
                           mu_r_ref[blk * N + n:blk * N + n + 1, :]) for blk in range(3))
        for blk in range(3):
            car_r[blk * N + n:blk * N + n + 1, :] = raw[blk][Tc - 1:Tc, :]
        a = jax.nn.sigmoid(a0_ref[n:n + 1, :] + BT[n])
        kk = k * kkw_ref[n:n + 1, :]
        ss = ss + kk * kk
        log_w = -DECAY_SCALE * jax.nn.sigmoid(w0_ref[n:n + 1, :] + W[n])
        cum = log_w
        shift = 1
        while shift < Tc:
            cum = cum + jnp.where(t_idx >= shift, pltpu.roll(cum, shift, axis=0), 0.0)
            shift *= 2
        p_inv = jnp.exp(-cum)
        R[n] = r * jnp.exp(cum)
        VP[value_rows(n)] = v
        KK[n] = kk * jnp.exp(cum - log_w)
        K[n] = k * (1.0 + (a - 1.0) * ka_ref[n:n + 1, :]) * p_inv
        BT[n] = kk * a * p_inv
        PL[n:n + 1, :] = jnp.exp(cum[Tc - 1:Tc, :])
    inv = 1.0 / jnp.maximum(jnp.sqrt(ss), 1e-12)
    for n in range(N):
        KK[n] = KK[n] * inv
        BT[n] = BT[n] * inv

    def step(t, carry):
        row = pl.ds(t, 1)
        blk = pl.ds(pl.multiple_of(t * SUBLANES, SUBLANES), SUBLANES)
        acc = [[None, None] for _ in range(VH)]
        for key in range(N):
            kk_row = KK[key, row, :]
            for vh in range(VH):
                term = S[key, vh] * kk_row
                prev = acc[vh][key % 2]
                acc[vh][key % 2] = term if prev is None else prev + term
        sa = [-(acc[vh][0] + acc[vh][1]) for vh in range(VH)]
        vv = [VP[vh, blk, :] for vh in range(VH)]
        out = [[None, None] for _ in range(VH)]
        for key in range(N):
            b_row = BT[key, row, :]
            k_row = K[key, row, :]
            r_row = R[key, row, :]
            for vh in range(VH):
                s_new = S[key, vh] + sa[vh] * b_row + vv[vh] * k_row
                S[key, vh] = s_new
                term = s_new * r_row
                prev = out[vh][key % 2]
                out[vh][key % 2] = term if prev is None else prev + term
        for vh in range(VH):
            OP[vh, blk, :] = out[vh][0] + out[vh][1]
        return carry

    lax.fori_loop(0, Tc, step, 0, unroll=8)
    for key in range(N):
        for vh in range(VH):
            S[key, vh] = S[key, vh] * PL[key:key + 1, :]

    inv_n = 1.0 / N
    tot = jnp.zeros((Tc, LANES), F32)
    bon = jnp.zeros((Tc, LANES), F32)
    for n in range(N):
        tot = tot + OP[value_rows(n)]
        bon = bon + R[n] * K[n] * rk_ref[n:n + 1, :]
    mean = tot * inv_n
    var = jnp.zeros((Tc, LANES), F32)
    for n in range(N):
        d = OP[value_rows(n)] - mean
        var = var + d * d
    rstd = lax.rsqrt(var * inv_n + GN_EPS)
    for j in range(NT):
        ys = []
        for g in range(NB):
            n = NB * j + g
            o = (OP[value_rows(n)] - mean) * rstd * gnw_ref[n:n + 1, :] + gnb_ref[n:n + 1, :]
            ys.append(o + bon * VP[value_rows(n)])
        for b, y in enumerate(_block_transpose(ys, RH)):
            y_ref[b, :, tile(j)] = (y * LG[b * NT + j]).astype(BF16)

    @pl.when(c == pl.num_programs(1) - 1)
    def _():
        s1_ref[...] = S[...]
        for b in range(NB):
            shn_r_ref[b] = zr_ref[b, Tc - 1:Tc, :]
        shn_g_ref[...] = car_g[...]
        shn_s_ref[...] = car_s[...]


def rwkv_branch(z, shp_r, sh_g, sh_s, S0, P, steps):
    B, T, _ = z.shape
    N = R_HEAD
    NB = RW_BATCH
    D = LANES // NB * N
    NT = D // LANES
    VH = N // SUBLANES
    assert B % NB == 0 and T % steps == 0 and steps % SUBLANES == 0
    g_blk = 8 * D // GATE_BLOCK

    def seq(width, j):
        return pl.BlockSpec((NB, steps, width), lambda g, c: (g, c, j))

    def per_batch(*shape):
        return pl.BlockSpec((NB,) + shape, lambda g, c: (g,) + (0,) * len(shape))

    def per_group(*shape):
        return pl.BlockSpec((None,) + shape, lambda g, c: (g,) + (0,) * len(shape))

    def const(*shape):
        return pl.BlockSpec(shape, lambda g, c: (0,) * len(shape))

    packed = pltpu.VMEM((N, steps, LANES), F32)
    natural = pltpu.VMEM((NB * NT, steps, LANES), F32)
    by_value = pltpu.VMEM((VH, steps * SUBLANES, LANES), F32)
    y, shn_r, shn_g, shn_s, S1 = pl.pallas_call(
        _rwkv_kernel,
        grid=(B // NB, T // steps),
        in_specs=[seq(3 * D, 1), seq(GATE_BLOCK, g_blk), seq(GATE_BLOCK, g_blk + 1),
                  per_group(3 * N, LANES), per_batch(1, GATE_BLOCK), per_batch(1, GATE_BLOCK),
                  const(3 * N, LANES), const(1, GATE_BLOCK), const(1, GATE_BLOCK),
                  const(GATE_BLOCK, D), const(GATE_BLOCK, D), const(GATE_BLOCK, D)]
                 + [const(N, LANES)] * 7
                 + [per_group(N, VH, SUBLANES, LANES)],
        out_specs=[seq(D, 0), per_batch(1, 3 * D), per_batch(1, GATE_BLOCK),
                   per_batch(1, GATE_BLOCK), per_group(N, VH, SUBLANES, LANES)],
        out_shape=[jax.ShapeDtypeStruct((B, T, D), BF16),
                   jax.ShapeDtypeStruct((B, 1, 3 * D), F32),
                   jax.ShapeDtypeStruct((B, 1, GATE_BLOCK), F32),
                   jax.ShapeDtypeStruct((B, 1, GATE_BLOCK), F32),
                   jax.ShapeDtypeStruct((B // NB, N, VH, SUBLANES, LANES), F32)],
        scratch_shapes=[pltpu.VMEM((N, VH, SUBLANES, LANES), F32),
                        pltpu.VMEM((3 * N, LANES), F32), pltpu.VMEM((NB, 1, GATE_BLOCK), F32),
                        pltpu.VMEM((NB, 1, GATE_BLOCK), F32), pltpu.VMEM((N, LANES), F32)]
                       + [natural] * 3 + [packed] * 5 + [by_value] * 2,
        compiler_params=_params("parallel", "arbitrary"),
        name="rwkv7",
    )(z, z, z, shp_r, sh_g, sh_s, P["mu_r"], P["mu_g"], P["mu_s"],
      P["w_w2"], P["w_a2"], P["w_g2"], P["w0"], P["a0"], P["k_k"], P["k_a"], P["r_k"],
      P["gn_w"], P["gn_b"], S0)
    return y, shn_r, shn_g, shn_s, S1


def _nh(x):
    lead = x.shape[:-1]
    return x.reshape(lead + (-1, R_HEAD)).swapaxes(-1, -2).reshape(x.shape)


def _hn(x):
    lead = x.shape[:-1]
    return x.reshape(lead + (R_HEAD, -1)).swapaxes(-1, -2).reshape(x.shape)


def _lane_vec(x):
    return jnp.tile(x.reshape(-1, R_HEAD).T, (1, RW_BATCH))


def _pack_rows(x):
    B = x.shape[0]
    x = x.reshape(B // RW_BATCH, RW_BATCH, -1, R_HEAD)
    return x.transpose(0, 3, 1, 2).reshape(B // RW_BATCH, R_HEAD, LANES)


def _pack_state(S):
    B, RH = S.shape[:2]
    S = S.reshape(B // RW_BATCH, RW_BATCH, RH, R_HEAD // SUBLANES, SUBLANES, R_HEAD)
    return S.transpose(0, 5, 3, 4, 1, 2).reshape(B // RW_BATCH, R_HEAD, R_HEAD // SUBLANES,
                                                 SUBLANES, LANES)


def _unpack_state(S):
    G = S.shape[0]
    RH = LANES // RW_BATCH
    S = S.reshape(G, R_HEAD, R_HEAD // SUBLANES, SUBLANES, RW_BATCH, RH)
    return S.transpose(0, 4, 5, 2, 3, 1).reshape(G * RW_BATCH, RH, R_HEAD, R_HEAD)


def _pick_tile(n, candidates):
    for c in candidates:
        if n % c == 0:
            return c
    return n


def _repack_w_in(w, D, lora_w, lora_a, lora_g):
    H = M_HEADS
    o_end = 3 * D
    f_end = o_end + 2 * H
    zr = f_end
    xw = zr + 3 * D
    xa = xw + lora_w
    xg = xa + lora_a
    ga = xg + lora_g
    small = lora_w + lora_a + 2 * H
    assert lora_g == GATE_BLOCK and small <= GATE_BLOCK
    pad = jnp.zeros((w.shape[0], GATE_BLOCK - small), w.dtype)
    rkv = [_nh(w[:, zr + i * D:zr + (i + 1) * D]) for i in range(3)]
    return jnp.concatenate(
        [w[:, :o_end]] + rkv + [w[:, ga:ga + D], _nh(w[:, ga + D:ga + 2 * D]), w[:, xg:ga],
                                w[:, xw:xg], w[:, o_end:f_end], pad], axis=1)


def _layer(h, p_l, conv0, C0, n0, m0, shift0, S0, chunk, W, final, g_final):
    B, T, D = h.shape
    M = B * T
    tm = min(512, M)
    lora_w, lora_a, _ = W["lora"]
    gate_off = lora_w + lora_a

    h2 = h.reshape(M, D)
    z2 = norm_matmul(h2, W["g_mix"], W["w_in"], min(1024, M), W["w_in_tn"])
    z = z2.reshape(B, T, -1)

    y_a, conv_new, C1, n1, m1 = mlstm_branch(z, conv0, C0, n0, m0, W["conv_w"], W["conv_b"],
                                             W["gate_bias"], W["m_norm_g"], chunk, gate_off)

    shp_r = jnp.concatenate([_pack_rows(shift0[:, 0, i * D:(i + 1) * D]) for i in range(3)], axis=1)
    sh_g = shift0[..., 3 * D + gate_off:]
    sh_s = jnp.pad(shift0[..., 3 * D:3 * D + gate_off], ((0, 0), (0, 0), (0, GATE_BLOCK - gate_off)))
    y_b, shn_r, shn_g, shn_s, S1 = rwkv_branch(z, shp_r, sh_g, sh_s, _pack_state(S0), W["rwkv"],
                                               min(32, T))
    shift_new = jnp.concatenate([_hn(shn_r[..., i * D:(i + 1) * D]) for i in range(3)]
                                + [shn_s[..., :gate_off], shn_g], axis=-1)
    S1 = _unpack_state(S1)

    h2 = merge_project(h2, z2, y_a.reshape(M, D), y_b.reshape(M, D), W["w_out"], W["w_out_nh"],
                       min(256, M))
    h2 = ffn(h2, W["g_ffn"], W["w_gate"], W["w_up"], W["w_down"], tm, W["ffn_tf"])
    h2 = ple(h2, W["g_ple"], p_l.reshape(M, -1), W["w_ple_gate"], W["w_ple_in"], g_final, tm, final)
    return h2.reshape(B, T, D), (conv_new, C1, n1, m1, shift_new, S1)


def kernel(x_prompt, x_sample, p_prompt, p_sample, state_mlstm_C, state_mlstm_n, state_mlstm_m, state_mlstm_conv, state_rwkv_wkv, state_rwkv_shift, g_mix, w_in, conv_w, conv_b, b_i, b_f, m_norm_g, rwkv_mu, w0, w_w2, a0, w_a2, w_g2, k_k, k_a, r_k, gn_w, gn_b, w_out, g_ffn, w_gate, w_up, w_down, g_ple, w_ple_gate, w_ple_in, g_final):
    depth = w_in.shape[0]
    D = x_prompt.shape[-1]
    H = M_HEADS
    DV = D // H
    DK = DV // 2
    QK = H * DK
    RH = D // R_HEAD
    r_shift = state_rwkv_shift.shape[-1]
    lora = (w_w2.shape[1], w_a2.shape[1], w_g2.shape[1])
    n_pack = 8 * D + 2 * GATE_BLOCK
    w_in_tn = _pick_tile(n_pack, (1536, 768, 512, 256, 128))

    layers = []
    for l in range(depth):
        gate_bias = jnp.zeros((1, GATE_BLOCK), F32)
        gate_bias = gate_bias.at[0, lora[0] + lora[1]:lora[0] + lora[1] + 2 * H].set(
            jnp.concatenate([b_i[l], b_f[l]]))
        mu = rwkv_mu[l]
        lw, la = lora[0], lora[1]
        w_w2p = jnp.zeros((GATE_BLOCK, D), F32).at[:lw].set(_nh(w_w2[l]))
        w_a2p = jnp.zeros((GATE_BLOCK, D), F32).at[lw:lw + la].set(_nh(w_a2[l]))
        rwkv = dict(
            mu_r=jnp.concatenate([_lane_vec(mu[i * D:(i + 1) * D]) for i in range(3)]),
            mu_g=mu[3 * D + lw + la:][None],
            mu_s=jnp.pad(mu[3 * D:3 * D + lw + la], (0, GATE_BLOCK - lw - la))[None],
            w_w2=w_w2p.astype(BF16), w_a2=w_a2p.astype(BF16), w_g2=_nh(w_g2[l]).astype(BF16),
            w0=_lane_vec(w0[l]), a0=_lane_vec(a0[l]), k_k=_lane_vec(k_k[l]),
            k_a=_lane_vec(k_a[l]), r_k=_lane_vec(r_k[l].reshape(-1)),
            gn_w=_lane_vec(gn_w[l]), gn_b=_lane_vec(gn_b[l]))
        layers.append(dict(
            g_mix=g_mix[l][None], w_in=_repack_w_in(w_in[l], D, *lora).astype(BF16),
            w_in_tn=w_in_tn, lora=lora,
            conv_w=conv_w[l], conv_b=conv_b[l][None], gate_bias=gate_bias,
            m_norm_g=m_norm_g[l][None], rwkv=rwkv,
            w_out=w_out[l].astype(BF16),
            w_out_nh=_nh(w_out[l].T).T.astype(BF16), g_ffn=g_ffn[l][None],
            w_gate=w_gate[l].astype(BF16), w_up=w_up[l].astype(BF16),
            w_down=w_down[l].astype(BF16), ffn_tf=_pick_tile(w_gate.shape[2], (512, 256, 128)),
            g_ple=g_ple[l][None], w_ple_gate=w_ple_gate[l].astype(BF16),
            w_ple_in=w_ple_in[l].astype(BF16)))
    gf = g_final[None]

    def run(x, p, conv0, C0, n0, m0, shift0, S0, chunk):
        h = x
        outs = ([], [], [], [], [], [])
        for l in range(depth):
            h, st = _layer(h, p[l], conv0[l], C0[l], n0[l], m0[l], shift0[l], S0[l], chunk,
                           layers[l], l == depth - 1, gf)
            for lst, s in zip(outs, st):
                lst.append(s)
        return h, [jnp.stack(o, axis=0) for o in outs]

    Bp = x_prompt.shape[0]
    y_prompt, (conv_p, C_p, n_p, m_p, shift_p, wkv_p) = run(
        x_prompt, p_prompt,
        jnp.zeros((depth, Bp, CONV_W - 1, 2 * QK), F32),
        jnp.zeros((depth, Bp, H, DK, DV), F32),
        jnp.zeros((depth, Bp, H, DK), F32),
        jnp.zeros((depth, Bp, H), F32),
        jnp.zeros((depth, Bp, 1, r_shift), F32),
        jnp.zeros((depth, Bp, RH, R_HEAD, R_HEAD), F32),
        CHUNK)
    y_sample, (conv_s, C_s, n_s, m_s, shift_s, wkv_s) = run(
        x_sample, p_sample, state_mlstm_conv, state_mlstm_C, state_mlstm_n, state_mlstm_m,
        state_rwkv_shift, state_rwkv_wkv, x_sample.shape[1])
    return (y_prompt, y_sample, C_p, n_p, m_p, conv_p, wkv_p, shift_p,
            C_s, n_s, m_s, conv_s, wkv_s, shift_s)
```

```python
import functools

import jax
import jax.numpy as jnp
from jax import lax
from jax.experimental import pallas as pl
from jax.experimental.pallas import tpu as pltpu

F32 = jnp.float32
BF16 = jnp.bfloat16

EPS = 1e-6
GN_EPS = 64e-5
M_HEADS = 4
CONV_W = 4
R_HEAD = 64
CHUNK = 64
MLSTM_CHUNKS_PER_STEP = 4
LANES = 128
SUBLANES = 8
VMEM_LIMIT = 48 * 1024 * 1024

GATE_BLOCK = 256


def _params(*sem):
    return pltpu.CompilerParams(dimension_semantics=sem, vmem_limit_bytes=VMEM_LIMIT)


def _rms(x, g):
    return x * lax.rsqrt(jnp.mean(x * x, axis=-1, keepdims=True) + EPS) * g


def _log_sigmoid(x):
    return jnp.minimum(x, 0.0) - jnp.log1p(jnp.exp(-jnp.abs(x)))


def _norm_mm_kernel(x_ref, g_ref, w_ref, o_ref, u_ref):
    @pl.when(pl.program_id(1) == 0)
    def _():
        u_ref[...] = _rms(x_ref[...], g_ref[...]).astype(BF16)

    o_ref[...] = jnp.dot(u_ref[...], w_ref[...], preferred_element_type=F32)


def norm_matmul(x, g, w, tm, tn):
    M, K = x.shape
    N = w.shape[1]
    return pl.pallas_call(
        _norm_mm_kernel,
        grid=(M // tm, N // tn),
        in_specs=[pl.BlockSpec((tm, K), lambda i, j: (i, 0)),
                  pl.BlockSpec((1, K), lambda i, j: (0, 0)),
                  pl.BlockSpec((K, tn), lambda i, j: (0, j))],
        out_specs=pl.BlockSpec((tm, tn), lambda i, j: (i, j)),
        out_shape=jax.ShapeDtypeStruct((M, N), F32),
        scratch_shapes=[pltpu.VMEM((tm, K), BF16)],
        compiler_params=_params("parallel", "arbitrary"),
        name="norm_matmul",
    )(x, g, w)


def _merge_kernel(h_ref, ga_ref, gb_ref, ya_ref, yb_ref, wa_ref, wb_ref, o_ref):
    ma = (jax.nn.sigmoid(ga_ref[...]) * ya_ref[...]).astype(BF16)
    mb = (jax.nn.sigmoid(gb_ref[...]) * yb_ref[...]).astype(BF16)
    o_ref[...] = (h_ref[...] + jnp.dot(ma, wa_ref[...], preferred_element_type=F32)
                  + jnp.dot(mb, wb_ref[...], preferred_element_type=F32))


def merge_project(h, z, ya, yb, wa, wb, tm):
    M, D = h.shape
    row = pl.BlockSpec((tm, D), lambda i: (i, 0))
    wspec = pl.BlockSpec((D, D), lambda i: (0, 0), pipeline_mode=pl.Buffered(1))
    return pl.pallas_call(
        _merge_kernel,
        grid=(M // tm,),
        in_specs=[row,
                  pl.BlockSpec((tm, D), lambda i: (i, 6)),
                  pl.BlockSpec((tm, D), lambda i: (i, 7)),
                  row, row, wspec, wspec],
        out_specs=row,
        out_shape=jax.ShapeDtypeStruct((M, D), F32),
        compiler_params=_params("parallel"),
        name="merge_project",
    )(h, z, z, ya, yb, wa, wb)


def _ffn_kernel(h_ref, g_ref, wgu_ref, wd_ref, o_ref, u_ref, acc_ref):
    f = pl.program_id(1)

    @pl.when(f == 0)
    def _():
        u_ref[...] = _rms(h_ref[...], g_ref[...]).astype(BF16)
        acc_ref[...] = jnp.zeros_like(acc_ref)

    tf = wd_ref.shape[0]
    gu = jnp.dot(u_ref[...], wgu_ref[...], preferred_element_type=F32)
    gate = gu[:, :tf]
    up = gu[:, tf:]
    act = (gate * jax.nn.sigmoid(gate) * up).astype(BF16)
    acc_ref[...] += jnp.dot(act, wd_ref[...], preferred_element_type=F32)

    @pl.when(f == pl.num_programs(1) - 1)
    def _():
        o_ref[...] = h_ref[...] + acc_ref[...]


def _interleave_tiles(wg, wu, tf):
    D, F = wg.shape
    return jnp.stack([wg.reshape(D, F // tf, tf), wu.reshape(D, F // tf, tf)], axis=2).reshape(D, 2 * F)


def ffn(h, g, wgu, wd, tm, tf):
    M, D = h.shape
    F = wd.shape[0]
    return pl.pallas_call(
        _ffn_kernel,
        grid=(M // tm, F // tf),
        in_specs=[pl.BlockSpec((tm, D), lambda i, j: (i, 0)),
                  pl.BlockSpec((1, D), lambda i, j: (0, 0)),
                  pl.BlockSpec((D, 2 * tf), lambda i, j: (0, j)),
                  pl.BlockSpec((tf, D), lambda i, j: (j, 0))],
        out_specs=pl.BlockSpec((tm, D), lambda i, j: (i, 0)),
        out_shape=jax.ShapeDtypeStruct((M, D), F32),
        scratch_shapes=[pltpu.VMEM((tm, D), BF16), pltpu.VMEM((tm, D), F32)],
        compiler_params=_params("parallel", "arbitrary"),
        name="ffn",
    )(h, g, wgu, wd)


def _ple_kernel(h_ref, g_ref, p_ref, wpg_ref, wpi_ref, gf_ref, o_ref, *, final):
    h = h_ref[...]
    u = _rms(h, g_ref[...]).astype(BF16)
    gate = jnp.dot(u, wpg_ref[...], preferred_element_type=F32)
    emb = jnp.dot(p_ref[...].astype(BF16), wpi_ref[...], preferred_element_type=F32)
    out = h + jax.nn.sigmoid(gate) * emb
    if final:
        out = _rms(out, gf_ref[...])
    o_ref[...] = out


def ple(h, g, p, wpg, wpi, g_final, tm, final):
    M, D = h.shape
    P = p.shape[1]
    return pl.pallas_call(
        functools.partial(_ple_kernel, final=final),
        grid=(M // tm,),
        in_specs=[pl.BlockSpec((tm, D), lambda i: (i, 0)),
                  pl.BlockSpec((1, D), lambda i: (0, 0)),
                  pl.BlockSpec((tm, P), lambda i: (i, 0)),
                  pl.BlockSpec((D, D), lambda i: (0, 0), pipeline_mode=pl.Buffered(1)),
                  pl.BlockSpec((P, D), lambda i: (0, 0), pipeline_mode=pl.Buffered(1)),
                  pl.BlockSpec((1, D), lambda i: (0, 0))],
        out_specs=pl.BlockSpec((tm, D), lambda i: (i, 0)),
        out_shape=jax.ShapeDtypeStruct((M, D), F32),
        compiler_params=_params("parallel"),
        name="ple",
    )(h, g, p, wpg, wpi, g_final)


def _mlstm_kernel(qk_ref, v_ref, o_ref, gt_ref, conv0_ref, cw_ref, cb_ref, gbias_ref, mg_ref,
                  c0_ref, n0_ref, m0_ref,
                  y_ref, convn_ref, c1_ref, n1_ref, m1_ref,
                  xp, C, n, m, *, gate_off, L):
    c = pl.program_id(1)
    rows, D = qk_ref.shape
    H = M_HEADS
    DV = D // H
    DK = DV // 2
    QK = H * DK
    halo = CONV_W - 1
    top = SUBLANES - halo

    @pl.when(c == 0)
    def _():
        xp[top:SUBLANES, :] = conv0_ref[...]
        C[...] = c0_ref[...]
        n[...] = n0_ref[...]
        m[...] = m0_ref[...]

    row = lax.broadcasted_iota(jnp.int32, (L, L), 0)
    col = lax.broadcasted_iota(jnp.int32, (L, L), 1)
    tril = row >= col
    eye = row == col

    def conv_silu(c0):
        acc = cb_ref[:, c0:c0 + DK] + xp[top:top + L, c0:c0 + DK] * cw_ref[0:1, c0:c0 + DK]
        for j in range(1, CONV_W):
            acc = acc + xp[top + j:top + j + L, c0:c0 + DK] * cw_ref[j:j + 1, c0:c0 + DK]
        return acc * jax.nn.sigmoid(acc)

    def chunk(i, carry):
        rs = pl.ds(pl.multiple_of(i * L, L), L)
        xp[SUBLANES:SUBLANES + L, :] = qk_ref[rs, :]
        gt = gt_ref[rs, :] + gbias_ref[...]
        for h in range(H):
            head(h, rs, gt)
        xp[top:SUBLANES, :] = xp[top + L:SUBLANES + L, :]
        return carry

    def head(h, rs, gt):
        q = conv_silu(h * DK)
        k = conv_silu(QK + h * DK) * (DK ** -0.5)
        vb = v_ref[rs, h * DV:(h + 1) * DV].astype(BF16)
        li_c = gt[:, gate_off + h:gate_off + h + 1]
        lf_c = _log_sigmoid(gt[:, gate_off + H + h:gate_off + H + h + 1])
        li_r = jnp.sum(jnp.where(eye, li_c, 0.0), axis=0, keepdims=True)
        b_row = jnp.sum(jnp.where(row <= col, lf_c, 0.0), axis=0, keepdims=True)
        b_col = jnp.sum(jnp.where(eye, b_row, 0.0), axis=1, keepdims=True)
        m_old = m[h]
        a_col = b_col + m_old
        dmat = jnp.where(tril, b_col - b_row + li_r, -jnp.inf)
        mt = jnp.maximum(a_col, jnp.max(dmat, axis=1, keepdims=True))
        ea = jnp.exp(a_col - mt)
        p = jnp.exp(dmat - mt)

        qb = q.astype(BF16)
        Ch = C[h]
        nh = n[h]
        s = lax.dot_general(qb, k.astype(BF16), (((1,), (1,)), ((), ())),
                            preferred_element_type=F32) * p
        num = (ea * jnp.dot(qb, Ch.astype(BF16), preferred_element_type=F32)
               + jnp.dot(s.astype(BF16), vb, preferred_element_type=F32))
        den = ea * jnp.sum(q * nh, axis=1, keepdims=True) + jnp.sum(s, axis=1, keepdims=True)
        hh = num / jnp.maximum(jnp.abs(den), jnp.exp(-mt))
        hh = hh * lax.rsqrt(jnp.mean(hh * hh, axis=1, keepdims=True) + EPS)
        y_ref[rs, h * DV:(h + 1) * DV] = (jax.nn.sigmoid(o_ref[rs, h * DV:(h + 1) * DV])
                                          * (hh * mg_ref[:, h * DV:(h + 1) * DV]))

        m_new = mt[L - 1:L, :]
        dec = jnp.exp(a_col[L - 1:L, :] - m_new)
        wk = jnp.exp(b_col[L - 1:L, :] - b_col + li_c - m_new)
        kw = k * wk
        C[h] = dec * Ch + lax.dot_general(kw.astype(BF16), vb, (((0,), (0,)), ((), ())),
                                          preferred_element_type=F32)
        n[h] = dec * nh + jnp.sum(kw, axis=0, keepdims=True)
        m[h] = m_new

    lax.fori_loop(0, rows // L, chunk, 0)

    @pl.when(c == pl.num_programs(1) - 1)
    def _():
        convn_ref[...] = xp[top:SUBLANES, :]
        c1_ref[...] = C[...]
        n1_ref[...] = n[...]
        m1_ref[...] = m[...]


def mlstm_branch(z, conv0, C0, n0, m0, conv_w, conv_b, gate_bias, m_norm_g, L, gate_off):
    B, T, _ = z.shape
    H = M_HEADS
    _, _, DK, DV = C0.shape
    D = H * DV
    halo = CONV_W - 1
    rows = L * _pick_tile(T // L, (MLSTM_CHUNKS_PER_STEP, 2, 1))

    def seq(j):
        return pl.BlockSpec((None, rows, D), lambda b, c: (b, c, j))

    def per_batch(*shape):
        return pl.BlockSpec((None,) + shape, lambda b, c: (b,) + (0,) * len(shape))

    def const(*shape):
        return pl.BlockSpec(shape, lambda b, c: (0,) * len(shape))

    gate_blk = (8 * D + GATE_BLOCK) // GATE_BLOCK
    y, conv_new, C1, n1, m1 = pl.pallas_call(
        functools.partial(_mlstm_kernel, gate_off=gate_off, L=L),
        grid=(B, T // rows),
        in_specs=[seq(0), seq(1), seq(2),
                  pl.BlockSpec((None, rows, GATE_BLOCK), lambda b, c: (b, c, gate_blk)),
                  per_batch(halo, D), const(CONV_W, D), const(1, D), const(1, GATE_BLOCK),
                  const(1, D),
                  per_batch(H, DK, DV), per_batch(H, 1, DK), per_batch(H, 1, 1)],
        out_specs=[seq(0), per_batch(halo, D),
                   per_batch(H, DK, DV), per_batch(H, 1, DK), per_batch(H, 1, 1)],
        out_shape=[jax.ShapeDtypeStruct((B, T, D), F32),
                   jax.ShapeDtypeStruct((B, halo, D), F32),
                   jax.ShapeDtypeStruct((B, H, DK, DV), F32),
                   jax.ShapeDtypeStruct((B, H, 1, DK), F32),
                   jax.ShapeDtypeStruct((B, H, 1, 1), F32)],
        scratch_shapes=[pltpu.VMEM((L + SUBLANES, D), F32), pltpu.VMEM((H, DK, DV), F32),
                        pltpu.VMEM((H, 1, DK), F32), pltpu.VMEM((H, 1, 1), F32)],
        compiler_params=_params("parallel", "arbitrary"),
        name="mlstm",
    )(z, z, z, z, conv0, conv_w, conv_b, gate_bias, m_norm_g,
      C0, n0[:, :, None, :], m0[:, :, None, None])
    return y, conv_new, C1, n1[:, :, 0, :], m1[:, :, 0, 0]


RW_BATCH = 4
DECAY_SCALE = 0.6065306597126334


def _block_transpose(a, rh):
    blk = lax.broadcasted_iota(jnp.int32, a[0].shape, 1) // rh
    out = []
    for g in range(4):
        acc = None
        for b in range(4):
            src = a[b] if b == g else pltpu.roll(a[b], ((b - g) % 4) * rh, axis=1)
            acc = src if acc is None else jnp.where(blk == b, src, acc)
        out.append(acc)
    return tuple(out)


def _rwkv_kernel(zr_ref, zg_ref, zs_ref, shp_r_ref, sh_g_ref, sh_s_ref,
                 mu_r_ref, mu_g_ref, mu_s_ref, ww2_ref, wa2_ref, wg2_ref,
                 w0_ref, a0_ref, kkw_ref, ka_ref, rk_ref, gnw_ref, gnb_ref, s0_ref,
                 y_ref, shn_r_ref, shn_g_ref, shn_s_ref, s1_ref,
                 S, car_r, car_g, car_s, PL, LW, LA, LG, R, W, K, KK, BT, VP, OP):
    c = pl.program_id(1)
    NB, Tc, D3 = zr_ref.shape
    D = D3 // 3
    RH = D // R_HEAD
    N = R_HEAD
    NT = D // LANES
    VH = N // SUBLANES

    @pl.when(c == 0)
    def _():
        S[...] = s0_ref[...]
        car_r[...] = shp_r_ref[...]
        car_g[...] = sh_g_ref[...]
        car_s[...] = sh_s_ref[...]

    first_row = lax.broadcasted_iota(jnp.int32, (Tc, LANES), 0) == 0

    def shifted(x, carry_row, mu):
        prev = jnp.where(first_row, carry_row, pltpu.roll(x, 1, axis=0))
        return x + mu * (prev - x)

    def tile(j):
        return slice(j * LANES, (j + 1) * LANES)

    def value_rows(n):
        return (n // SUBLANES, pl.ds(n % SUBLANES, Tc, stride=SUBLANES), slice(None))

    for bi in range(NB):
        xs = [shifted(zs_ref[bi, :, tile(j)], car_s[bi, :, tile(j)], mu_s_ref[:, tile(j)])
              for j in range(GATE_BLOCK // LANES)]
        xg = [shifted(zg_ref[bi, :, tile(j)], car_g[bi, :, tile(j)], mu_g_ref[:, tile(j)])
              for j in range(GATE_BLOCK // LANES)]
        s_small = jnp.concatenate(xs, axis=1)
        s_gate = jnp.concatenate(xg, axis=1)
        lw = jnp.dot(jnp.tanh(s_small).astype(BF16), ww2_ref[...], preferred_element_type=F32)
        la = jnp.dot(s_small.astype(BF16), wa2_ref[...], preferred_element_type=F32)
        lg = jnp.dot(jax.nn.sigmoid(s_gate).astype(BF16), wg2_ref[...],
                     preferred_element_type=F32)
        for j in range(NT):
            LW[bi * NT + j] = lw[:, tile(j)]
            LA[bi * NT + j] = la[:, tile(j)]
            LG[bi * NT + j] = lg[:, tile(j)]
        car_g[bi] = zg_ref[bi, Tc - 1:Tc, :]
        car_s[bi] = zs_ref[bi, Tc - 1:Tc, :]

    for j in range(NT):
        for blk, dst in enumerate((R, K, None)):
            packed = _block_transpose([zr_ref[b, :, tile(blk * NT + j)] for b in range(NB)], RH)
            for g in range(NB):
                if dst is None:
                    VP[value_rows(NB * j + g)] = packed[g]
                else:
                    dst[NB * j + g] = packed[g]
        for src, dst in ((LW, W), (LA, BT)):
            packed = _block_transpose([src[b * NT + j] for b in range(NB)], RH)
            for g in range(NB):
                dst[NB * j + g] = packed[g]

    t_idx = lax.broadcasted_iota(jnp.int32, (Tc, LANES), 0)
    ss = jnp.zeros((Tc, LANES), F32)
    for n in range(N):
        raw = (R[n], K[n], VP[value_rows(n)])
        r, k, v = (shifted(raw[blk], car_r[blk * N + n:blk * N + n + 1, :],
                           mu_r_ref[blk * N + n:blk * N + n + 1, :]) for blk in range(3))
        for blk in range(3):
            car_r[blk * N + n:blk * N + n + 1, :] = raw[blk][Tc - 1:Tc, :]
        a = jax.nn.sigmoid(a0_ref[n:n + 1, :] + BT[n])
        kk = k * kkw_ref[n:n + 1, :]
        ss = ss + kk * kk
        log_w = -DECAY_SCALE * jax.nn.sigmoid(w0_ref[n:n + 1, :] + W[n])
        cum = log_w
        shift = 1
        while shift < Tc:
            cum = cum + jnp.where(t_idx >= shift, pltpu.roll(cum, shift, axis=0), 0.0)
            shift *= 2
        p_inv = jnp.exp(-cum)
        R[n] = r * jnp.exp(cum)
        VP[value_rows(n)] = v
        KK[n] = kk * jnp.exp(cum - log_w)
        K[n] = k * (1.0 + (a - 1.0) * ka_ref[n:n + 1, :]) * p_inv
        BT[n] = kk * a * p_inv
        PL[n:n + 1, :] = jnp.exp(cum[Tc - 1:Tc, :])
    inv = 1.0 / jnp.maximum(jnp.sqrt(ss), 1e-12)
    for n in range(N):
        KK[n] = KK[n] * inv
        BT[n] = BT[n] * inv

    def step(t, carry):
        row = pl.ds(t, 1)
        blk = pl.ds(pl.multiple_of(t * SUBLANES, SUBLANES), SUBLANES)
        acc = [[None, None] for _ in range(VH)]
        for key in range(N):
            kk_row = KK[key, row, :]
            for vh in range(VH):
                term = S[key, vh] * kk_row
                prev = acc[vh][key % 2]
                acc[vh][key % 2] = term if prev is None else prev + term
        sa = [-(acc[vh][0] + acc[vh][1]) for vh in range(VH)]
        vv = [VP[vh, blk, :] for vh in range(VH)]
        out = [[None, None] for _ in range(VH)]
        for key in range(N):
            b_row = BT[key, row, :]
            k_row = K[key, row, :]
            r_row = R[key, row, :]
            for vh in range(VH):
                s_new = S[key, vh] + sa[vh] * b_row + vv[vh] * k_row
                S[key, vh] = s_new
                term = s_new * r_row
                prev = out[vh][key % 2]
                out[vh][key % 2] = term if prev is None else prev + term
        for vh in range(VH):
            OP[vh, blk, :] = out[vh][0] + out[vh][1]
        return carry

    lax.fori_loop(0, Tc, step, 0, unroll=8)
    for key in range(N):
        for vh in range(VH):
            S[key, vh] = S[key, vh] * PL[key:key + 1, :]

    inv_n = 1.0 / N
    tot = jnp.zeros((Tc, LANES), F32)
    bon = jnp.zeros((Tc, LANES), F32)
    for n in range(N):
        tot = tot + OP[value_rows(n)]
        bon = bon + R[n] * K[n] * rk_ref[n:n + 1, :]
    mean = tot * inv_n
    var = jnp.zeros((Tc, LANES), F32)
    for n in range(N):
        d = OP[value_rows(n)] - mean
        var = var + d * d
    rstd = lax.rsqrt(var * inv_n + GN_EPS)
    for j in range(NT):
        ys = []
        for g in range(NB):
            n = NB * j + g
            o = (OP[value_rows(n)] - mean) * rstd * gnw_ref[n:n + 1, :] + gnb_ref[n:n + 1, :]
            ys.append(o + bon * VP[value_rows(n)])
        for b, y in enumerate(_block_transpose(ys, RH)):
            y_ref[b, :, tile(j)] = y * LG[b * NT + j]

    @pl.when(c == pl.num_programs(1) - 1)
    def _():
        s1_ref[...] = S[...]
        for b in range(NB):
            shn_r_ref[b] = zr_ref[b, Tc - 1:Tc, :]
        shn_g_ref[...] = car_g[...]
        shn_s_ref[...] = car_s[...]


def rwkv_branch(z, shp_r, sh_g, sh_s, S0, P, steps):
    B, T, _ = z.shape
    N = R_HEAD
    NB = RW_BATCH
    D = LANES // NB * N
    NT = D // LANES
    VH = N // SUBLANES
    assert B % NB == 0 and T % steps == 0 and steps % SUBLANES == 0
    g_blk = 8 * D // GATE_BLOCK

    def seq(width, j):
        return pl.BlockSpec((NB, steps, width), lambda g, c: (g, c, j))

    def per_batch(*shape):
        return pl.BlockSpec((NB,) + shape, lambda g, c: (g,) + (0,) * len(shape))

    def per_group(*shape):
        return pl.BlockSpec((None,) + shape, lambda g, c: (g,) + (0,) * len(shape))

    def const(*shape):
        return pl.BlockSpec(shape, lambda g, c: (0,) * len(shape))

    packed = pltpu.VMEM((N, steps, LANES), F32)
    natural = pltpu.VMEM((NB * NT, steps, LANES), F32)
    by_value = pltpu.VMEM((VH, steps * SUBLANES, LANES), F32)
    y, shn_r, shn_g, shn_s, S1 = pl.pallas_call(
        _rwkv_kernel,
        grid=(B // NB, T // steps),
        in_specs=[seq(3 * D, 1), seq(GATE_BLOCK, g_blk), seq(GATE_BLOCK, g_blk + 1),
                  per_group(3 * N, LANES), per_batch(1, GATE_BLOCK), per_batch(1, GATE_BLOCK),
                  const(3 * N, LANES), const(1, GATE_BLOCK), const(1, GATE_BLOCK),
                  const(GATE_BLOCK, D), const(GATE_BLOCK, D), const(GATE_BLOCK, D)]
                 + [const(N, LANES)] * 7
                 + [per_group(N, VH, SUBLANES, LANES)],
        out_specs=[seq(D, 0), per_batch(1, 3 * D), per_batch(1, GATE_BLOCK),
                   per_batch(1, GATE_BLOCK), per_group(N, VH, SUBLANES, LANES)],
        out_shape=[jax.ShapeDtypeStruct((B, T, D), F32),
                   jax.ShapeDtypeStruct((B, 1, 3 * D), F32),
                   jax.ShapeDtypeStruct((B, 1, GATE_BLOCK), F32),
                   jax.ShapeDtypeStruct((B, 1, GATE_BLOCK), F32),
                   jax.ShapeDtypeStruct((B // NB, N, VH, SUBLANES, LANES), F32)],
        scratch_shapes=[pltpu.VMEM((N, VH, SUBLANES, LANES), F32),
                        pltpu.VMEM((3 * N, LANES), F32), pltpu.VMEM((NB, 1, GATE_BLOCK), F32),
                        pltpu.VMEM((NB, 1, GATE_BLOCK), F32), pltpu.VMEM((N, LANES), F32)]
                       + [natural] * 3 + [packed] * 5 + [by_value] * 2,
        compiler_params=_params("parallel", "arbitrary"),
        name="rwkv7",
    )(z, z, z, shp_r, sh_g, sh_s, P["mu_r"], P["mu_g"], P["mu_s"],
      P["w_w2"], P["w_a2"], P["w_g2"], P["w0"], P["a0"], P["k_k"], P["k_a"], P["r_k"],
      P["gn_w"], P["gn_b"], S0)
    return y, shn_r, shn_g, shn_s, S1


def _nh(x):
    lead = x.shape[:-1]
    return x.reshape(lead + (-1, R_HEAD)).swapaxes(-1, -2).reshape(x.shape)


def _hn(x):
    lead = x.shape[:-1]
    return x.reshape(lead + (R_HEAD, -1)).swapaxes(-1, -2).reshape(x.shape)


def _lane_vec(x):
    return jnp.tile(x.reshape(-1, R_HEAD).T, (1, RW_BATCH))


def _pack_rows(x):
    B = x.shape[0]
    x = x.reshape(B // RW_BATCH, RW_BATCH, -1, R_HEAD)
    return x.transpose(0, 3, 1, 2).reshape(B // RW_BATCH, R_HEAD, LANES)


def _pack_state(S):
    B, RH = S.shape[:2]
    S = S.reshape(B // RW_BATCH, RW_BATCH, RH, R_HEAD // SUBLANES, SUBLANES, R_HEAD)
    return S.transpose(0, 5, 3, 4, 1, 2).reshape(B // RW_BATCH, R_HEAD, R_HEAD // SUBLANES,
                                                 SUBLANES, LANES)


def _unpack_state(S):
    G = S.shape[0]
    RH = LANES // RW_BATCH
    S = S.reshape(G, R_HEAD, R_HEAD // SUBLANES, SUBLANES, RW_BATCH, RH)
    return S.transpose(0, 4, 5, 2, 3, 1).reshape(G * RW_BATCH, RH, R_HEAD, R_HEAD)


def _pick_tile(n, candidates):
    for c in candidates:
        if n % c == 0:
            return c
    return n


def _repack_w_in(w, D, lora_w, lora_a, lora_g):
    H = M_HEADS
    o_end = 3 * D
    f_end = o_end + 2 * H
    zr = f_end
    xw = zr + 3 * D
    xa = xw + lora_w
    xg = xa + lora_a
    ga = xg + lora_g
    small = lora_w + lora_a + 2 * H
    assert lora_g == GATE_BLOCK and small <= GATE_BLOCK
    pad = jnp.zeros((w.shape[0], GATE_BLOCK - small), w.dtype)
    rkv = [_nh(w[:, zr + i * D:zr + (i + 1) * D]) for i in range(3)]
    return jnp.concatenate(
        [w[:, :o_end]] + rkv + [w[:, ga:ga + D], _nh(w[:, ga + D:ga + 2 * D]), w[:, xg:ga],
                                w[:, xw:xg], w[:, o_end:f_end], pad], axis=1)


def _layer(h, p_l, conv0, C0, n0, m0, shift0, S0, chunk, W, final, g_final):
    B, T, D = h.shape
    M = B * T
    tm = min(512, M)
    lora_w, lora_a, _ = W["lora"]
    gate_off = lora_w + lora_a

    h2 = h.reshape(M, D)
    z2 = norm_matmul(h2, W["g_mix"], W["w_in"], min(1024, M), W["w_in_tn"])
    z = z2.reshape(B, T, -1)

    y_a, conv_new, C1, n1, m1 = mlstm_branch(z, conv0, C0, n0, m0, W["conv_w"], W["conv_b"],
                                             W["gate_bias"], W["m_norm_g"], chunk, gate_off)

    shp_r = jnp.concatenate([_pack_rows(shift0[:, 0, i * D:(i + 1) * D]) for i in range(3)], axis=1)
    sh_g = shift0[..., 3 * D + gate_off:]
    sh_s = jnp.pad(shift0[..., 3 * D:3 * D + gate_off], ((0, 0), (0, 0), (0, GATE_BLOCK - gate_off)))
    y_b, shn_r, shn_g, shn_s, S1 = rwkv_branch(z, shp_r, sh_g, sh_s, _pack_state(S0), W["rwkv"],
                                               min(32, T))
    shift_new = jnp.concatenate([_hn(shn_r[..., i * D:(i + 1) * D]) for i in range(3)]
                                + [shn_s[..., :gate_off], shn_g], axis=-1)
    S1 = _unpack_state(S1)

    h2 = merge_project(h2, z2, y_a.reshape(M, D), y_b.reshape(M, D), W["w_out"], W["w_out_nh"],
                       min(256, M))
    h2 = ffn(h2, W["g_ffn"], W["w_gate_up"], W["w_down"], tm, W["ffn_tf"])
    h2 = ple(h2, W["g_ple"], p_l.reshape(M, -1), W["w_ple_gate"], W["w_ple_in"], g_final, tm, final)
    return h2.reshape(B, T, D), (conv_new, C1, n1, m1, shift_new, S1)


def kernel(x_prompt, x_sample, p_prompt, p_sample, state_mlstm_C, state_mlstm_n, state_mlstm_m, state_mlstm_conv, state_rwkv_wkv, state_rwkv_shift, g_mix, w_in, conv_w, conv_b, b_i, b_f, m_norm_g, rwkv_mu, w0, w_w2, a0, w_a2, w_g2, k_k, k_a, r_k, gn_w, gn_b, w_out, g_ffn, w_gate, w_up, w_down, g_ple, w_ple_gate, w_ple_in, g_final):
    depth = w_in.shape[0]
    D = x_prompt.shape[-1]
    H = M_HEADS
    DV = D // H
    DK = DV // 2
    QK = H * DK
    RH = D // R_HEAD
    r_shift = state_rwkv_shift.shape[-1]
    lora = (w_w2.shape[1], w_a2.shape[1], w_g2.shape[1])
    n_pack = 8 * D + 2 * GATE_BLOCK
    w_in_tn = _pick_tile(n_pack, (1536, 768, 512, 256, 128))

    ffn_tf = _pick_tile(w_gate.shape[2], (512, 256, 128))
    layers = []
    for l in range(depth):
        gate_bias = jnp.zeros((1, GATE_BLOCK), F32)
        gate_bias = gate_bias.at[0, lora[0] + lora[1]:lora[0] + lora[1] + 2 * H].set(
            jnp.concatenate([b_i[l], b_f[l]]))
        mu = rwkv_mu[l]
        lw, la = lora[0], lora[1]
        w_w2p = jnp.zeros((GATE_BLOCK, D), F32).at[:lw].set(_nh(w_w2[l]))
        w_a2p = jnp.zeros((GATE_BLOCK, D), F32).at[lw:lw + la].set(_nh(w_a2[l]))
        rwkv = dict(
            mu_r=jnp.concatenate([_lane_vec(mu[i * D:(i + 1) * D]) for i in range(3)]),
            mu_g=mu[3 * D + lw + la:][None],
            mu_s=jnp.pad(mu[3 * D:3 * D + lw + la], (0, GATE_BLOCK - lw - la))[None],
            w_w2=w_w2p.astype(BF16), w_a2=w_a2p.astype(BF16), w_g2=_nh(w_g2[l]).astype(BF16),
            w0=_lane_vec(w0[l]), a0=_lane_vec(a0[l]), k_k=_lane_vec(k_k[l]),
            k_a=_lane_vec(k_a[l]), r_k=_lane_vec(r_k[l].reshape(-1)),
            gn_w=_lane_vec(gn_w[l]), gn_b=_lane_vec(gn_b[l]))
        layers.append(dict(
            g_mix=g_mix[l][None], w_in=_repack_w_in(w_in[l], D, *lora).astype(BF16),
            w_in_tn=w_in_tn, lora=lora,
            conv_w=conv_w[l], conv_b=conv_b[l][None], gate_bias=gate_bias,
            m_norm_g=m_norm_g[l][None], rwkv=rwkv,
            w_out=w_out[l].astype(BF16),
            w_out_nh=_nh(w_out[l].T).T.astype(BF16), g_ffn=g_ffn[l][None],
            w_gate_up=_interleave_tiles(w_gate[l].astype(BF16), w_up[l].astype(BF16), ffn_tf),
            w_down=w_down[l].astype(BF16), ffn_tf=ffn_tf,
            g_ple=g_ple[l][None], w_ple_gate=w_ple_gate[l].astype(BF16),
            w_ple_in=w_ple_in[l].astype(BF16)))
    gf = g_final[None]

    def run(x, p, conv0, C0, n0, m0, shift0, S0, chunk):
        h = x
        outs = ([], [], [], [], [], [])
        for l in range(depth):
            h, st = _layer(h, p[l], conv0[l], C0[l], n0[l], m0[l], shift0[l], S0[l], chunk,
                           layers[l], l == depth - 1, gf)
            for lst, s in zip(outs, st):
                lst.append(s)
        return h, [jnp.stack(o, axis=0) for o in outs]

    Bp = x_prompt.shape[0]
    y_prompt, (conv_p, C_p, n_p, m_p, shift_p, wkv_p) = run(
        x_prompt, p_prompt,
        jnp.zeros((depth, Bp, CONV_W - 1, 2 * QK), F32),
        jnp.zeros((depth, Bp, H, DK, DV), F32),
        jnp.zeros((depth, Bp, H, DK), F32),
        jnp.zeros((depth, Bp, H), F32),
        jnp.zeros((depth, Bp, 1, r_shift), F32),
        jnp.zeros((depth, Bp, RH, R_HEAD, R_HEAD), F32),
        CHUNK)
    y_sample, (conv_s, C_s, n_s, m_s, shift_s, wkv_s) = run(
        x_sample, p_sample, state_mlstm_conv, state_mlstm_C, state_mlstm_n, state_mlstm_m,
        state_rwkv_shift, state_rwkv_wkv, x_sample.shape[1])
    return (y_prompt, y_sample, C_p, n_p, m_p, conv_p, wkv_p, shift_p,
            C_s, n_s, m_s, conv_s, wkv_s, shift_s)
```
